```python
import jax, jax.numpy as jnp
from jax import lax
import numpy as np

D_MODEL = 2048
BATCH = 4
SEQ = 2048
DEPTH = 2

N_MIXERS = 2
N_MLA = (DEPTH + 1) // 2
N_CONV = DEPTH // 2
N_HEADS = 16
Q_LORA = 512
KV_LORA = 512
D_NOPE = 128
D_ROPE = 64
D_V = 128
QK_DIM = D_NOPE + D_ROPE
ROPE_THETA = 10000.0
Q_BLOCK = 128
CONV_CH = D_MODEL
CONV_WIDTH = 31
D_FF = 5632
FFN_RESIDUAL_WEIGHT = 0.5
D_PLE = 256
EPS = 1e-6

kernel_name = "mla_conformer_macaron_hybrid"


def rms_norm(x, g):
    xf = x.astype(jnp.float32)
    y = xf * lax.rsqrt(jnp.mean(xf * xf, axis=-1, keepdims=True) + EPS)
    return (y * g.astype(jnp.float32)).astype(x.dtype)


def layer_norm(x, g, b):
    xf = x.astype(jnp.float32)
    mu = jnp.mean(xf, axis=-1, keepdims=True)
    xc = xf - mu
    y = xc * lax.rsqrt(jnp.mean(xc * xc, axis=-1, keepdims=True) + EPS)
    return (y * g.astype(jnp.float32) + b.astype(jnp.float32)).astype(x.dtype)


def swiglu_ffn(h, w_in, w_out):
    g, u = jnp.split(h @ w_in, 2, axis=-1)
    return (jax.nn.silu(g) * u) @ w_out


def rope_tables(positions):
    inv_freq = ROPE_THETA ** (-jnp.arange(0, D_ROPE, 2, dtype=jnp.float32) / D_ROPE)
    ang = positions.astype(jnp.float32)[..., None] * inv_freq
    return jnp.cos(ang)[:, :, None, :], jnp.sin(ang)[:, :, None, :]


def apply_rope_tail(x, cos, sin):
    x_nope, x_rope = x[..., :D_NOPE], x[..., D_NOPE:]
    xr = x_rope.astype(jnp.float32)
    x1, x2 = xr[..., : D_ROPE // 2], xr[..., D_ROPE // 2:]
    rot = jnp.concatenate([x1 * cos - x2 * sin, x2 * cos + x1 * sin], axis=-1)
    return jnp.concatenate([x_nope, rot.astype(x.dtype)], axis=-1)


def causal_blocked_attention(q, k, v):
    S = q.shape[1]
    scale = QK_DIM ** -0.5
    outs = []
    for blk in range(S // Q_BLOCK):
        s0 = blk * Q_BLOCK
        n_keys = s0 + Q_BLOCK
        q_b = q[:, s0:s0 + Q_BLOCK]
        k_b = k[:, :n_keys]
        v_b = v[:, :n_keys]
        scores = jnp.einsum('bqhe,bkhe->bhqk', q_b, k_b).astype(jnp.float32) * scale
        q_idx = s0 + jnp.arange(Q_BLOCK)[:, None]
        k_idx = jnp.arange(n_keys)[None, :]
        scores = jnp.where(k_idx <= q_idx, scores, -jnp.inf)
        probs = jax.nn.softmax(scores, axis=-1).astype(v.dtype)
        outs.append(jnp.einsum('bhqk,bkhd->bqhd', probs, v_b))
    return jnp.concatenate(outs, axis=1)


def mla_mixer(h, positions, w_in, q_lat_norm, kv_lat_norm, w_uq, w_ukv, q_gain, k_gain, w_o):
    B, S, _ = h.shape
    lat = h @ w_in
    c_q, c_kv, k_rope = jnp.split(lat, [Q_LORA, Q_LORA + KV_LORA], axis=-1)
    c_q = rms_norm(c_q, q_lat_norm)
    c_kv = rms_norm(c_kv, kv_lat_norm)
    q = (c_q @ w_uq).reshape(B, S, N_HEADS, QK_DIM)
    kv = (c_kv @ w_ukv).reshape(B, S, N_HEADS, D_NOPE + D_V)
    k_nope, v = jnp.split(kv, [D_NOPE], axis=-1)
    k_rope = jnp.broadcast_to(k_rope[:, :, None, :], (B, S, N_HEADS, D_ROPE))
    k = jnp.concatenate([k_nope, k_rope], axis=-1)
    q = rms_norm(q, q_gain)
    k = rms_norm(k, k_gain)
    cos, sin = rope_tables(positions)
    q = apply_rope_tail(q, cos, sin)
    k = apply_rope_tail(k, cos, sin)
    o = causal_blocked_attention(q, k, v)
    return o.reshape(B, S, N_HEADS * D_V) @ w_o


def conv_mixer(h, w_pw1, b_pw1, w_dw, b_dw, ln_g, ln_b, w_pw2):
    a, g = jnp.split(h @ w_pw1 + b_pw1, 2, axis=-1)
    u = a * jax.nn.sigmoid(g)
    u = lax.conv_general_dilated(
        u, w_dw[:, None, :], window_strides=(1,), padding=[(CONV_WIDTH - 1, 0)],
        dimension_numbers=('NWC', 'WIO', 'NWC'), feature_group_count=CONV_CH) + b_dw
    u = jax.nn.silu(layer_norm(u, ln_g, ln_b))
    return u @ w_pw2


def setup_inputs(seed: int = 0) -> dict:
    key = jax.random.key(seed)
    ks = iter(jax.random.split(key, 40))

    def w(shape, fan_in):
        return jax.random.normal(next(ks), shape, jnp.float32) * (fan_in ** -0.5)

    def gain(shape):
        return 1.0 + 0.02 * jax.random.normal(next(ks), shape, jnp.float32)

    def bias(shape):
        return 0.01 * jax.random.normal(next(ks), shape, jnp.float32)

    x = jax.random.normal(next(ks), (BATCH, SEQ, D_MODEL), jnp.float32)
    p = jax.random.normal(next(ks), (DEPTH, BATCH, SEQ, D_PLE), jnp.float32)
    positions = jnp.broadcast_to(jnp.arange(SEQ, dtype=jnp.int32)[None, :], (BATCH, SEQ))
    return {
        "x": x, "p": p, "positions": positions,
        "ffn_a_norm": gain((DEPTH, D_MODEL)),
        "ffn_a_w_in": w((DEPTH, D_MODEL, 2 * D_FF), D_MODEL),
        "ffn_a_w_out": w((DEPTH, D_FF, D_MODEL), D_FF),
        "ffn_b_norm": gain((DEPTH, D_MODEL)),
        "ffn_b_w_in": w((DEPTH, D_MODEL, 2 * D_FF), D_MODEL),
        "ffn_b_w_out": w((DEPTH, D_FF, D_MODEL), D_FF),
        "mix_norm": gain((DEPTH, D_MODEL)),
        "mla_w_in": w((N_MLA, D_MODEL, Q_LORA + KV_LORA + D_ROPE), D_MODEL),
        "mla_q_lat_norm": gain((N_MLA, Q_LORA)),
        "mla_kv_lat_norm": gain((N_MLA, KV_LORA)),
        "mla_w_uq": w((N_MLA, Q_LORA, N_HEADS * QK_DIM), Q_LORA),
        "mla_w_ukv": w((N_MLA, KV_LORA, N_HEADS * (D_NOPE + D_V)), KV_LORA),
        "mla_q_gain": gain((N_MLA, QK_DIM)),
        "mla_k_gain": gain((N_MLA, QK_DIM)),
        "mla_w_o": w((N_MLA, N_HEADS * D_V, D_MODEL), N_HEADS * D_V),
        "conv_w_pw1": w((N_CONV, D_MODEL, 2 * CONV_CH), D_MODEL),
        "conv_b_pw1": bias((N_CONV, 2 * CONV_CH)),
        "conv_w_dw": w((N_CONV, CONV_WIDTH, CONV_CH), CONV_WIDTH),
        "conv_b_dw": bias((N_CONV, CONV_CH)),
        "conv_ln_g": gain((N_CONV, CONV_CH)),
        "conv_ln_b": bias((N_CONV, CONV_CH)),
        "conv_w_pw2": w((N_CONV, CONV_CH, D_MODEL), CONV_CH),
        "ple_w_proj": w((DEPTH, D_PLE, D_MODEL), D_PLE),
        "ple_norm": gain((DEPTH, D_MODEL)),
        "ple_gate_norm": gain((DEPTH, D_MODEL)),
        "ple_w_gate": w((DEPTH, D_MODEL, D_MODEL), D_MODEL),
    }


def reference(x, p, positions,
              ffn_a_norm, ffn_a_w_in, ffn_a_w_out,
              ffn_b_norm, ffn_b_w_in, ffn_b_w_out,
              mix_norm,
              mla_w_in, mla_q_lat_norm, mla_kv_lat_norm, mla_w_uq, mla_w_ukv,
              mla_q_gain, mla_k_gain, mla_w_o,
              conv_w_pw1, conv_b_pw1, conv_w_dw, conv_b_dw, conv_ln_g, conv_ln_b, conv_w_pw2,
              ple_w_proj, ple_norm, ple_gate_norm, ple_w_gate):
    h = x
    for i in range(DEPTH):
        h = h + FFN_RESIDUAL_WEIGHT * swiglu_ffn(rms_norm(h, ffn_a_norm[i]), ffn_a_w_in[i], ffn_a_w_out[i])
        hn = rms_norm(h, mix_norm[i])
        j = i // N_MIXERS
        if i % N_MIXERS == 0:
            h = h + mla_mixer(hn, positions, mla_w_in[j], mla_q_lat_norm[j], mla_kv_lat_norm[j],
                              mla_w_uq[j], mla_w_ukv[j], mla_q_gain[j], mla_k_gain[j], mla_w_o[j])
        else:
            h = h + conv_mixer(hn, conv_w_pw1[j], conv_b_pw1[j], conv_w_dw[j], conv_b_dw[j],
                               conv_ln_g[j], conv_ln_b[j], conv_w_pw2[j])
        h = h + FFN_RESIDUAL_WEIGHT * swiglu_ffn(rms_norm(h, ffn_b_norm[i]), ffn_b_w_in[i], ffn_b_w_out[i])
        e = rms_norm(p[i] @ ple_w_proj[i], ple_norm[i])
        gate = jax.nn.sigmoid(rms_norm(h, ple_gate_norm[i]) @ ple_w_gate[i])
        h = h + e * gate
    return h
```

```python
import functools
import math

import jax
import jax.numpy as jnp
from jax import lax
from jax.experimental import pallas as pl
from jax.experimental.pallas import tpu as pltpu

D_MODEL = 2048
N_HEADS = 16
Q_LORA = 512
KV_LORA = 512
D_NOPE = 128
D_ROPE = 64
D_V = 128
QK_DIM = D_NOPE + D_ROPE
ROPE_THETA = 10000.0
CONV_WIDTH = 31
D_FF = 5632
D_PLE = 256
EPS = 1e-6
FFN_RESIDUAL_WEIGHT = 0.5

LANES = 128
V7X_VMEM_LIMIT_BYTES = 60 * 1024 * 1024

QK_PAD = 2 * LANES
CONV_HALO = 32

F32 = jnp.float32
BF16 = jnp.bfloat16


def _bf(x):
    return x.astype(BF16)


def _dot(a, b):
    return jnp.dot(a, b, preferred_element_type=F32)


def _rms(x, g):
    ms = jnp.mean(x * x, axis=-1, keepdims=True)
    return x * lax.rsqrt(ms + EPS) * g


def _params(sem):
    return pltpu.CompilerParams(dimension_semantics=sem,
                                vmem_limit_bytes=V7X_VMEM_LIMIT_BYTES)


def _const_spec(shape, layer=None):
    nd = len(shape)
    if layer is None:
        return pl.BlockSpec(shape, lambda *_: (0,) * nd, pipeline_mode=pl.Buffered(1))
    return pl.BlockSpec((None,) + shape, lambda *_: (layer,) + (0,) * nd,
                        pipeline_mode=pl.Buffered(1))


def _ffn_kernel(x_ref, g_ref, wg_ref, wu_ref, wo_ref, o_ref, xn_ref, acc_ref, *, n_f):
    j = pl.program_id(1)

    @pl.when(j == 0)
    def _():
        xn_ref[...] = _bf(_rms(x_ref[...], g_ref[...]))
        acc_ref[...] = jnp.zeros_like(acc_ref)

    xn = xn_ref[...]
    g = _dot(xn, _bf(wg_ref[...]))
    u = _dot(xn, _bf(wu_ref[...]))
    a = _bf(g * jax.nn.sigmoid(g) * u)
    acc_ref[...] += _dot(a, _bf(wo_ref[...]))

    @pl.when(j == n_f - 1)
    def _():
        o_ref[...] = x_ref[...] + FFN_RESIDUAL_WEIGHT * acc_ref[...]


def _ffn(x, norm_g, w_in, w_out, layer, *, tm=1024, tf=256):
    m, d = x.shape
    n_f = D_FF // tf
    return pl.pallas_call(
        functools.partial(_ffn_kernel, n_f=n_f),
        out_shape=jax.ShapeDtypeStruct((m, d), F32),
        grid=(m // tm, n_f),
        in_specs=[
            pl.BlockSpec((tm, d), lambda i, j: (i, 0), pipeline_mode=pl.Buffered(1)),
            pl.BlockSpec((None, 1, d), lambda i, j: (layer, 0, 0)),
            pl.BlockSpec((None, d, tf), lambda i, j: (layer, 0, j)),
            pl.BlockSpec((None, d, tf), lambda i, j: (layer, 0, j + n_f)),
            pl.BlockSpec((None, tf, d), lambda i, j: (layer, j, 0)),
        ],
        out_specs=pl.BlockSpec((tm, d), lambda i, j: (i, 0)),
        scratch_shapes=[pltpu.VMEM((tm, d), BF16), pltpu.VMEM((tm, d), F32)],
        compiler_params=_params(("parallel", "arbitrary")),
        name="ffn",
    )(x, norm_g.reshape(-1, 1, d), w_in, w_in, w_out)


def _swap_rope_halves(x, lo_mask):
    return jnp.where(lo_mask, pltpu.roll(x, LANES - D_ROPE // 2, 1),
                     pltpu.roll(x, D_ROPE // 2, 1))


def _mla_proj_kernel(x_ref, pos_ref, mixg_ref, win_ref, qlg_ref, kvlg_ref, wuq_ref, wukv_ref,
                     qg_ref, kg_ref, q_ref, k_ref, v_ref):
    xn = _bf(_rms(x_ref[...], mixg_ref[...]))
    lat = _dot(xn, win_ref[...])
    c_q = _bf(_rms(lat[:, :Q_LORA], qlg_ref[...]))
    c_kv = _bf(_rms(lat[:, Q_LORA:Q_LORA + KV_LORA], kvlg_ref[...]))
    k_rope = lat[:, Q_LORA + KV_LORA:]

    tm = x_ref.shape[0]
    half = D_ROPE // 2
    lane = lax.broadcasted_iota(jnp.int32, (tm, LANES), 1)
    freq_idx = (lane % half).astype(F32)
    inv_freq = jnp.exp(freq_idx * (-2.0 / D_ROPE * math.log(ROPE_THETA)))
    ang = pos_ref[...].astype(F32) * inv_freq
    lo_mask = (lane % D_ROPE) < half
    cos = jnp.cos(ang)
    sin = jnp.sin(ang)
    sin = jnp.where(lo_mask, -sin, sin)

    def rope(t):
        return t * cos + _swap_rope_halves(t, lo_mask) * sin

    qg = qg_ref[...]
    kg = kg_ref[...]
    k_rot = rope(k_rope * kg[:, D_NOPE:])
    kr_ss = jnp.sum(k_rope * k_rope, axis=-1, keepdims=True)
    scale = QK_DIM ** -0.5
    for h in range(N_HEADS):
        cols = slice(h * QK_PAD, (h + 1) * QK_PAD)
        qh = _dot(c_q, wuq_ref[:, cols])
        rq = lax.rsqrt(jnp.sum(qh * qh, axis=-1, keepdims=True) * (1.0 / QK_DIM) + EPS) * scale
        qh = qh * qg
        q_ref[0, h, :, :D_NOPE] = _bf(qh[:, :D_NOPE] * rq)
        q_ref[0, h, :, D_NOPE:] = _bf(rope(qh[:, D_NOPE:]) * rq)

        kvh = _dot(c_kv, wukv_ref[:, cols])
        kn = kvh[:, :D_NOPE]
        rk = lax.rsqrt((jnp.sum(kn * kn, axis=-1, keepdims=True) + kr_ss) * (1.0 / QK_DIM) + EPS)
        k_ref[0, h, :, :D_NOPE] = _bf(kn * kg[:, :D_NOPE] * rk)
        k_ref[0, h, :, D_NOPE:] = _bf(k_rot * rk)
        v_ref[0, h, :, :] = _bf(kvh[:, D_NOPE:])


def _mla_proj(x, pos, mix_g, w_in, q_lat_g, kv_lat_g, w_uq, w_ukv, q_gain, k_gain, *, batch, tm=512):
    m, d = x.shape
    seq = m // batch
    n_s = seq // tm
    lat_w = Q_LORA + KV_LORA + LANES
    w_in_p = _bf(jnp.pad(w_in, ((0, 0), (0, lat_w - w_in.shape[1]))))
    w_uq_p = _bf(jnp.pad(w_uq.reshape(Q_LORA, N_HEADS, QK_DIM),
                         ((0, 0), (0, 0), (0, QK_PAD - QK_DIM))).reshape(Q_LORA, N_HEADS * QK_PAD))
    qg_p = jnp.pad(q_gain, (0, QK_PAD - QK_DIM)).reshape(1, QK_PAD)
    kg_p = jnp.pad(k_gain, (0, QK_PAD - QK_DIM)).reshape(1, QK_PAD)
    hm = lambda w: pl.BlockSpec((1, N_HEADS, tm, w), lambda i: (i // n_s, 0, i % n_s, 0))
    return pl.pallas_call(
        _mla_proj_kernel,
        out_shape=(jax.ShapeDtypeStruct((batch, N_HEADS, seq, QK_PAD), BF16),
                   jax.ShapeDtypeStruct((batch, N_HEADS, seq, QK_PAD), BF16),
                   jax.ShapeDtypeStruct((batch, N_HEADS, seq, D_V), BF16)),
        grid=(m // tm,),
        in_specs=[
            pl.BlockSpec((tm, d), lambda i: (i, 0)),
            pl.BlockSpec((tm, 1), lambda i: (i, 0)),
            _const_spec((1, d)),
            _const_spec((d, lat_w)),
            _const_spec((1, Q_LORA)),
            _const_spec((1, KV_LORA)),
            _const_spec((Q_LORA, N_HEADS * QK_PAD)),
            _const_spec((KV_LORA, N_HEADS * (D_NOPE + D_V))),
            _const_spec((1, QK_PAD)),
            _const_spec((1, QK_PAD)),
        ],
        out_specs=(hm(QK_PAD), hm(QK_PAD), hm(D_V)),
        compiler_params=_params(("parallel",)),
        name="mla_proj",
    )(x, pos.reshape(m, 1), mix_g.reshape(1, d), w_in_p, q_lat_g.reshape(1, Q_LORA),
      kv_lat_g.reshape(1, KV_LORA), w_uq_p, _bf(w_ukv), qg_p, kg_p)


def _attn_kernel(q_ref, k_ref, v_ref, o_ref, *, seq, tq):
    for qi in range(seq // tq):
        n_keys = (qi + 1) * tq
        q = q_ref[0, 0, qi * tq:(qi + 1) * tq, :]
        k = k_ref[0, 0, :n_keys, :]
        s = lax.dot_general(q, k, (((1,), (1,)), ((), ())), preferred_element_type=F32)
        q_idx = qi * tq + lax.broadcasted_iota(jnp.int32, (tq, n_keys), 0)
        k_idx = lax.broadcasted_iota(jnp.int32, (tq, n_keys), 1)
        s = jnp.where(k_idx <= q_idx, s, -jnp.inf)
        p = jnp.exp(s - jnp.max(s, axis=-1, keepdims=True))
        l = jnp.sum(p, axis=-1, keepdims=True)
        o = _dot(_bf(p), v_ref[0, 0, :n_keys, :])
        o_ref[0, qi * tq:(qi + 1) * tq, :] = _bf(o / l)


def _attention(q, k, v, *, tq=512):
    b, h, s, _ = q.shape
    return pl.pallas_call(
        functools.partial(_attn_kernel, seq=s, tq=tq),
        out_shape=jax.ShapeDtypeStruct((b, s, h * D_V), BF16),
        grid=(b, h),
        in_specs=[
            pl.BlockSpec((1, 1, s, QK_PAD), lambda i, j: (i, j, 0, 0)),
            pl.BlockSpec((1, 1, s, QK_PAD), lambda i, j: (i, j, 0, 0)),
            pl.BlockSpec((1, 1, s, D_V), lambda i, j: (i, j, 0, 0)),
        ],
        out_specs=pl.BlockSpec((1, s, D_V), lambda i, j: (i, 0, j)),
        compiler_params=_params(("parallel", "parallel")),
        name="attention",
    )(q, k, v)


def _proj_res_kernel(a_ref, w_ref, h_ref, o_ref):
    o_ref[...] = h_ref[...] + _dot(a_ref[...], _bf(w_ref[...]))


def _proj_residual(a, w, h, *, tm=512):
    m, d = h.shape
    k = a.shape[1]
    return pl.pallas_call(
        _proj_res_kernel,
        out_shape=jax.ShapeDtypeStruct((m, d), F32),
        grid=(m // tm,),
        in_specs=[
            pl.BlockSpec((tm, k), lambda i: (i, 0)),
            _const_spec((k, d)),
            pl.BlockSpec((tm, d), lambda i: (i, 0)),
        ],
        out_specs=pl.BlockSpec((tm, d), lambda i: (i, 0)),
        compiler_params=_params(("parallel",)),
        name="attn_out_proj",
    )(a, w, h)


def _glu_kernel(x_ref, g_ref, wa_ref, wg_ref, ba_ref, bg_ref, u_ref, xn_ref):
    @pl.when(pl.program_id(1) == 0)
    def _():
        xn_ref[...] = _bf(_rms(x_ref[...], g_ref[...]))

    xn = xn_ref[...]
    a = _dot(xn, _bf(wa_ref[...])) + ba_ref[...]
    g = _dot(xn, _bf(wg_ref[...])) + bg_ref[...]
    u_ref[...] = a * jax.nn.sigmoid(g)


def _glu(x, norm_g, w, b, *, tm=1024, tn=512):
    m, d = x.shape
    c = w.shape[1] // 2
    n_c = c // tn
    b2 = b.reshape(1, 2 * c)
    return pl.pallas_call(
        _glu_kernel,
        out_shape=jax.ShapeDtypeStruct((m, c), F32),
        grid=(m // tm, n_c),
        in_specs=[
            pl.BlockSpec((tm, d), lambda i, j: (i, 0)),
            pl.BlockSpec((1, d), lambda i, j: (0, 0)),
            pl.BlockSpec((d, tn), lambda i, j: (0, j)),
            pl.BlockSpec((d, tn), lambda i, j: (0, j + n_c)),
            pl.BlockSpec((1, tn), lambda i, j: (0, j)),
            pl.BlockSpec((1, tn), lambda i, j: (0, j + n_c)),
        ],
        out_specs=pl.BlockSpec((tm, tn), lambda i, j: (i, j)),
        scratch_shapes=[pltpu.VMEM((tm, d), BF16)],
        compiler_params=_params(("parallel", "arbitrary")),
        name="conv_glu",
    )(x, norm_g.reshape(1, d), w, w, b2, b2)


def _conv_kernel(u_ref, halo_ref, wdw_ref, bdw_ref, lng_ref, lnb_ref, w2_ref, h_ref, o_ref,
                 ext_ref, y_ref, *, ts, rb, cb):
    i = pl.program_id(1)
    halo = halo_ref[0]
    ext_ref[:CONV_HALO, :] = jnp.where(i > 0, halo, jnp.zeros_like(halo))
    ext_ref[CONV_HALO:, :] = u_ref[0]

    off = CONV_HALO - (CONV_WIDTH - 1)
    c = u_ref.shape[2]
    for c0 in range(0, c, cb):
        cs = slice(c0, c0 + cb)
        for r0 in range(0, ts, rb):
            acc = jnp.zeros((rb, cb), F32) + bdw_ref[:, cs]
            for t in range(CONV_WIDTH):
                acc = acc + ext_ref[r0 + off + t:r0 + off + t + rb, cs] * wdw_ref[t:t + 1, cs]
            y_ref[r0:r0 + rb, cs] = acc

    y = y_ref[...]
    mu = jnp.mean(y, axis=-1, keepdims=True)
    yc = y - mu
    yn = yc * lax.rsqrt(jnp.mean(yc * yc, axis=-1, keepdims=True) + EPS) * lng_ref[...] + lnb_ref[...]
    act = _bf(yn * jax.nn.sigmoid(yn))
    o_ref[0] = h_ref[0] + _dot(act, _bf(w2_ref[...]))


def _conv_module(u, w_dw, b_dw, ln_g, ln_b, w2, h, *, ts=256, rb=64, cb=256):
    b, s, c = u.shape
    d = w2.shape[1]
    r = ts // CONV_HALO
    row = lambda i, j: (i, j, 0)
    return pl.pallas_call(
        functools.partial(_conv_kernel, ts=ts, rb=rb, cb=cb),
        out_shape=jax.ShapeDtypeStruct((b, s, d), F32),
        grid=(b, s // ts),
        in_specs=[
            pl.BlockSpec((1, ts, c), row),
            pl.BlockSpec((1, CONV_HALO, c), lambda i, j: (i, jnp.maximum(j * r - 1, 0), 0)),
            _const_spec((CONV_WIDTH, c)),
            _const_spec((1, c)),
            _const_spec((1, c)),
            _const_spec((1, c)),
            _const_spec((c, d)),
            pl.BlockSpec((1, ts, d), row),
        ],
        out_specs=pl.BlockSpec((1, ts, d), row),
        scratch_shapes=[pltpu.VMEM((ts + CONV_HALO, c), F32), pltpu.VMEM((ts, c), F32)],
        compiler_params=_params(("parallel", "parallel")),
        name="conv_module",
    )(u, u, w_dw, b_dw.reshape(1, c), ln_g.reshape(1, c), ln_b.reshape(1, c), w2, h)


def _ple_kernel(h_ref, p_ref, wp_ref, eg_ref, gg_ref, wg_ref, o_ref):
    h = h_ref[...]
    e = _rms(_dot(_bf(p_ref[...]), _bf(wp_ref[...])), eg_ref[...])
    gate = jax.nn.sigmoid(_dot(_bf(_rms(h, gg_ref[...])), _bf(wg_ref[...])))
    o_ref[...] = h + e * gate


def _ple(h, p, w_proj, e_norm, gate_norm, w_gate, layer, *, tm=512):
    m, d = h.shape
    return pl.pallas_call(
        _ple_kernel,
        out_shape=jax.ShapeDtypeStruct((m, d), F32),
        grid=(m // tm,),
        in_specs=[
            pl.BlockSpec((tm, d), lambda i: (i, 0)),
            pl.BlockSpec((None, tm, D_PLE), lambda i: (layer, i, 0)),
            _const_spec((D_PLE, d), layer),
            _const_spec((1, d), layer),
            _const_spec((1, d), layer),
            _const_spec((d, d), layer),
        ],
        out_specs=pl.BlockSpec((tm, d), lambda i: (i, 0)),
        compiler_params=_params(("parallel",)),
        name="ple",
    )(h, p, w_proj, e_norm.reshape(-1, 1, d), gate_norm.reshape(-1, 1, d), w_gate)


def kernel(x, p, positions, ffn_a_norm, ffn_a_w_in, ffn_a_w_out, ffn_b_norm, ffn_b_w_in, ffn_b_w_out,
           mix_norm, mla_w_in, mla_q_lat_norm, mla_kv_lat_norm, mla_w_uq, mla_w_ukv, mla_q_gain,
           mla_k_gain, mla_w_o, conv_w_pw1, conv_b_pw1, conv_w_dw, conv_b_dw, conv_ln_g, conv_ln_b,
           conv_w_pw2, ple_w_proj, ple_norm, ple_gate_norm, ple_w_gate):
    batch, seq, d = x.shape
    m = batch * seq
    depth = ffn_a_norm.shape[0]
    h = x.reshape(m, d)
    pos = positions.reshape(m)
    p2 = p.reshape(depth, m, -1)
    for i in range(depth):
        h = _ffn(h, ffn_a_norm, ffn_a_w_in, ffn_a_w_out, i)
        j = i // 2
        if i % 2 == 0:
            q, k, v = _mla_proj(h, pos, mix_norm[i], mla_w_in[j], mla_q_lat_norm[j], mla_kv_lat_norm[j],
                                mla_w_uq[j], mla_w_ukv[j], mla_q_gain[j], mla_k_gain[j], batch=batch)
            o = _attention(q, k, v)
            h = _proj_residual(o.reshape(m, N_HEADS * D_V), mla_w_o[j], h)
        else:
            u = _glu(h, mix_norm[i], conv_w_pw1[j], conv_b_pw1[j])
            h = _conv_module(u.reshape(batch, seq, -1), conv_w_dw[j], conv_b_dw[j], conv_ln_g[j],
                             conv_ln_b[j], conv_w_pw2[j], h.reshape(batch, seq, d)).reshape(m, d)
        h = _ffn(h, ffn_b_norm, ffn_b_w_in, ffn_b_w_out, i)
        h = _ple(h, p2, ple_w_proj, ple_norm, ple_gate_norm, ple_w_gate, i)
    return h.reshape(batch, seq, d)
```

```python
import functools
import math

import jax
import jax.numpy as jnp
from jax import lax
from jax.experimental import pallas as pl
from jax.experimental.pallas import tpu as pltpu

D_MODEL = 2048
N_HEADS = 16
Q_LORA = 512
KV_LORA = 512
D_NOPE = 128
D_ROPE = 64
D_V = 128
QK_DIM = D_NOPE + D_ROPE
ROPE_THETA = 10000.0
CONV_WIDTH = 31
D_FF = 5632
D_PLE = 256
EPS = 1e-6
FFN_RESIDUAL_WEIGHT = 0.5

LANES = 128
SUBLANES = 8
V7X_VMEM_LIMIT_BYTES = 60 * 1024 * 1024

QK_PAD = 2 * LANES
CONV_HALO = 32

F32 = jnp.float32
BF16 = jnp.bfloat16


def _bf(x):
    return x.astype(BF16)


def _dot(a, b):
    return jnp.dot(a, b, preferred_element_type=F32)


def _rms(x, g):
    ms = jnp.mean(x * x, axis=-1, keepdims=True)
    return x * lax.rsqrt(ms + EPS) * g


def _params(sem):
    return pltpu.CompilerParams(dimension_semantics=sem,
                                vmem_limit_bytes=V7X_VMEM_LIMIT_BYTES)


def _const_spec(shape, layer=None):
    nd = len(shape)
    if layer is None:
        return pl.BlockSpec(shape, lambda *_: (0,) * nd, pipeline_mode=pl.Buffered(1))
    return pl.BlockSpec((None,) + shape, lambda *_: (layer,) + (0,) * nd,
                        pipeline_mode=pl.Buffered(1))


def _ffn_kernel(x_ref, g_ref, wg_ref, wu_ref, wo_ref, o_ref, xn_ref):
    @pl.when(pl.program_id(1) == 0)
    def _():
        x = x_ref[...]
        xn_ref[...] = _bf(_rms(x, g_ref[...]))
        o_ref[...] = x

    xn = xn_ref[...]
    g = _dot(xn, _bf(wg_ref[...]))
    u = _dot(xn, _bf(wu_ref[...]))
    a = _bf(g * jax.nn.sigmoid(g) * u)
    o_ref[...] += FFN_RESIDUAL_WEIGHT * _dot(a, _bf(wo_ref[...]))


def _ffn(x, norm_g, w_in, w_out, layer, *, tm=1024, tf=256):
    m, d = x.shape
    n_f = D_FF // tf
    return pl.pallas_call(
        _ffn_kernel,
        out_shape=jax.ShapeDtypeStruct((m, d), F32),
        grid=(m // tm, n_f),
        in_specs=[
            pl.BlockSpec((tm, d), lambda i, j: (i, 0), pipeline_mode=pl.Buffered(1)),
            pl.BlockSpec((None, 1, d), lambda i, j: (layer, 0, 0)),
            pl.BlockSpec((None, d, tf), lambda i, j: (layer, 0, j)),
            pl.BlockSpec((None, d, tf), lambda i, j: (layer, 0, j + n_f)),
            pl.BlockSpec((None, tf, d), lambda i, j: (layer, j, 0)),
        ],
        out_specs=pl.BlockSpec((tm, d), lambda i, j: (i, 0)),
        scratch_shapes=[pltpu.VMEM((tm, d), BF16)],
        compiler_params=_params(("parallel", "arbitrary")),
        name="ffn",
    )(x, norm_g.reshape(-1, 1, d), w_in, w_in, w_out)


def _swap_halves_cols(w):
    n = w.shape[-1] // 2
    return jnp.concatenate([w[..., n:], w[..., :n]], axis=-1)


def _row_sum_mxu(sq, ones):
    return _dot(_bf(sq), ones)


def _mla_proj_kernel(x_ref, pos_ref, mixg_ref, win_ref, qlg_ref, kvlg_ref, wuq_ref, wukv_ref,
                     qg_ref, kg_ref, ones_ref, q_ref, k_ref, v_ref):
    xn = _bf(_rms(x_ref[...], mixg_ref[...]))
    lat = _dot(xn, win_ref[...])
    c_q = _bf(_rms(lat[:, :Q_LORA], qlg_ref[...]))
    c_kv = _bf(_rms(lat[:, Q_LORA:Q_LORA + KV_LORA], kvlg_ref[...]))
    k_rope = lat[:, Q_LORA + KV_LORA:]

    tm = x_ref.shape[0]
    half = D_ROPE // 2
    lane = lax.broadcasted_iota(jnp.int32, (tm, LANES), 1)
    inv_freq = jnp.exp((lane % half).astype(F32) * (-2.0 / D_ROPE * math.log(ROPE_THETA)))
    ang = pos_ref[...].astype(F32) * inv_freq
    keep = lane < D_ROPE
    cos = jnp.where(keep, jnp.cos(ang), 0.0)
    sin = jnp.sin(ang)
    sin = jnp.where(keep, jnp.where(lane < half, -sin, sin), 0.0)

    def rope(t):
        return t * cos + pltpu.roll(t, D_ROPE, 1) * sin

    ones_q = ones_ref[...]
    ones_nope = ones_ref[:D_NOPE, :]
    ones_rope = ones_ref[D_NOPE:, :]

    qg = qg_ref[...]
    kg = kg_ref[...]
    k_rot = rope(k_rope * kg[:, D_NOPE:])
    kr_ss = _row_sum_mxu(k_rope * k_rope, ones_rope)
    scale = QK_DIM ** -0.5
    for h in range(N_HEADS):
        cols = slice(h * QK_PAD, (h + 1) * QK_PAD)
        qh = _dot(c_q, wuq_ref[:, cols])
        rq = lax.rsqrt(_row_sum_mxu(qh * qh, ones_q) * (1.0 / QK_DIM) + EPS) * scale
        qh = qh * qg
        q_ref[0, h, :, :D_NOPE] = _bf(qh[:, :D_NOPE] * rq)
        q_ref[0, h, :, D_NOPE:] = _bf(rope(qh[:, D_NOPE:]) * rq)

        kvh = _dot(c_kv, wukv_ref[:, cols])
        kn = kvh[:, :D_NOPE]
        rk = lax.rsqrt((_row_sum_mxu(kn * kn, ones_nope) + kr_ss) * (1.0 / QK_DIM) + EPS)
        k_ref[0, h, :, :D_NOPE] = _bf(kn * kg[:, :D_NOPE] * rk)
        k_ref[0, h, :, D_NOPE:] = _bf(k_rot * rk)
        v_ref[0, h, :, :] = _bf(kvh[:, D_NOPE:])


def _mla_proj(x, pos, mix_g, w_in, q_lat_g, kv_lat_g, w_uq, w_ukv, q_gain, k_gain, *, batch, tm=512):
    m, d = x.shape
    seq = m // batch
    n_s = seq // tm
    lat_w = Q_LORA + KV_LORA + LANES
    w_in_p = _bf(jnp.concatenate([w_in, _swap_halves_cols(w_in[:, Q_LORA + KV_LORA:])], axis=1))
    w_uq_h = w_uq.reshape(Q_LORA, N_HEADS, QK_DIM)
    w_uq_p = _bf(jnp.concatenate([w_uq_h, _swap_halves_cols(w_uq_h[..., D_NOPE:])], axis=-1)
                 .reshape(Q_LORA, N_HEADS * QK_PAD))
    qg_p = jnp.concatenate([q_gain, _swap_halves_cols(q_gain[D_NOPE:])]).reshape(1, QK_PAD)
    kg_p = jnp.concatenate([k_gain, _swap_halves_cols(k_gain[D_NOPE:])]).reshape(1, QK_PAD)
    ones = (lax.broadcasted_iota(jnp.int32, (QK_PAD, LANES), 0) < QK_DIM).astype(BF16)
    hm = lambda w: pl.BlockSpec((1, N_HEADS, tm, w), lambda i: (i // n_s, 0, i % n_s, 0))
    return pl.pallas_call(
        _mla_proj_kernel,
        out_shape=(jax.ShapeDtypeStruct((batch, N_HEADS, seq, QK_PAD), BF16),
                   jax.ShapeDtypeStruct((batch, N_HEADS, seq, QK_PAD), BF16),
                   jax.ShapeDtypeStruct((batch, N_HEADS, seq, D_V), BF16)),
        grid=(m // tm,),
        in_specs=[
            pl.BlockSpec((tm, d), lambda i: (i, 0)),
            pl.BlockSpec((tm, 1), lambda i: (i, 0)),
            _const_spec((1, d)),
            _const_spec((d, lat_w)),
            _const_spec((1, Q_LORA)),
            _const_spec((1, KV_LORA)),
            _const_spec((Q_LORA, N_HEADS * QK_PAD)),
            _const_spec((KV_LORA, N_HEADS * (D_NOPE + D_V))),
            _const_spec((1, QK_PAD)),
            _const_spec((1, QK_PAD)),
            _const_spec((QK_PAD, LANES)),
        ],
        out_specs=(hm(QK_PAD), hm(QK_PAD), hm(D_V)),
        compiler_params=_params(("parallel",)),
        name="mla_proj",
    )(x, pos.reshape(m, 1), mix_g.reshape(1, d), w_in_p, q_lat_g.reshape(1, Q_LORA),
      kv_lat_g.reshape(1, KV_LORA), w_uq_p, _bf(w_ukv), qg_p, kg_p, ones)


def _attn_kernel(q_ref, k_ref, v_ref, o_ref, *, seq, tq):
    for qi in range(seq // tq):
        n_keys = (qi + 1) * tq
        q = q_ref[0, 0, qi * tq:(qi + 1) * tq, :]
        k = k_ref[0, 0, :n_keys, :]
        s = lax.dot_general(q, k, (((1,), (1,)), ((), ())), preferred_element_type=F32)
        q_idx = qi * tq + lax.broadcasted_iota(jnp.int32, (tq, n_keys), 0)
        k_idx = lax.broadcasted_iota(jnp.int32, (tq, n_keys), 1)
        s = jnp.where(k_idx <= q_idx, s, -jnp.inf)
        p = jnp.exp(s - jnp.max(s, axis=-1, keepdims=True))
        l = jnp.sum(p, axis=-1, keepdims=True)
        o = _dot(_bf(p), v_ref[0, 0, :n_keys, :])
        o_ref[0, qi * tq:(qi + 1) * tq, :] = _bf(o / l)


def _attention(q, k, v, *, tq=512):
    b, h, s, _ = q.shape
    return pl.pallas_call(
        functools.partial(_attn_kernel, seq=s, tq=tq),
        out_shape=jax.ShapeDtypeStruct((b, s, h * D_V), BF16),
        grid=(b, h),
        in_specs=[
            pl.BlockSpec((1, 1, s, QK_PAD), lambda i, j: (i, j, 0, 0)),
            pl.BlockSpec((1, 1, s, QK_PAD), lambda i, j: (i, j, 0, 0)),
            pl.BlockSpec((1, 1, s, D_V), lambda i, j: (i, j, 0, 0)),
        ],
        out_specs=pl.BlockSpec((1, s, D_V), lambda i, j: (i, 0, j)),
        compiler_params=_params(("parallel", "parallel")),
        name="attention",
    )(q, k, v)


def _proj_res_kernel(a_ref, w_ref, h_ref, o_ref):
    o_ref[...] = h_ref[...] + _dot(a_ref[...], _bf(w_ref[...]))


def _proj_residual(a, w, h, *, tm=512):
    m, d = h.shape
    k = a.shape[1]
    return pl.pallas_call(
        _proj_res_kernel,
        out_shape=jax.ShapeDtypeStruct((m, d), F32),
        grid=(m // tm,),
        in_specs=[
            pl.BlockSpec((tm, k), lambda i: (i, 0)),
            _const_spec((k, d)),
            pl.BlockSpec((tm, d), lambda i: (i, 0)),
        ],
        out_specs=pl.BlockSpec((tm, d), lambda i: (i, 0)),
        compiler_params=_params(("parallel",)),
        name="attn_out_proj",
    )(a, w, h)


def _glu_kernel(x_ref, g_ref, wa_ref, wg_ref, ba_ref, bg_ref, u_ref, xn_ref):
    @pl.when(pl.program_id(1) == 0)
    def _():
        xn_ref[...] = _bf(_rms(x_ref[...], g_ref[...]))

    xn = xn_ref[...]
    a = _dot(xn, _bf(wa_ref[...])) + ba_ref[...]
    g = _dot(xn, _bf(wg_ref[...])) + bg_ref[...]
    u_ref[...] = a * jax.nn.sigmoid(g)


def _glu(x, norm_g, w, b, *, tm=1024, tn=512):
    m, d = x.shape
    c = w.shape[1] // 2
    n_c = c // tn
    b2 = b.reshape(1, 2 * c)
    return pl.pallas_call(
        _glu_kernel,
        out_shape=jax.ShapeDtypeStruct((m, c), F32),
        grid=(m // tm, n_c),
        in_specs=[
            pl.BlockSpec((tm, d), lambda i, j: (i, 0)),
            pl.BlockSpec((1, d), lambda i, j: (0, 0)),
            pl.BlockSpec((d, tn), lambda i, j: (0, j)),
            pl.BlockSpec((d, tn), lambda i, j: (0, j + n_c)),
            pl.BlockSpec((1, tn), lambda i, j: (0, j)),
            pl.BlockSpec((1, tn), lambda i, j: (0, j + n_c)),
        ],
        out_specs=pl.BlockSpec((tm, tn), lambda i, j: (i, j)),
        scratch_shapes=[pltpu.VMEM((tm, d), BF16)],
        compiler_params=_params(("parallel", "arbitrary")),
        name="conv_glu",
    )(x, norm_g.reshape(1, d), w, w, b2, b2)


def _conv_kernel(u_ref, halo_ref, wdw_ref, bdw_ref, lng_ref, lnb_ref, w2_ref, h_ref, o_ref,
                 ext_ref, sh_ref, y_ref, *, ts, rb, cb):
    i = pl.program_id(1)
    halo = halo_ref[0]
    ext_ref[:CONV_HALO, :] = jnp.where(i > 0, halo, jnp.zeros_like(halo))
    ext_ref[CONV_HALO:, :] = u_ref[0]

    off = CONV_HALO - (CONV_WIDTH - 1)
    n_sh = sh_ref.shape[1]
    c = u_ref.shape[2]
    for c0 in range(0, c, cb):
        cs = slice(c0, c0 + cb)
        for b in range(1, SUBLANES):
            sh_ref[b - 1] = ext_ref[b:b + n_sh, cs]
        for r0 in range(0, ts, rb):
            acc = jnp.zeros((rb // SUBLANES, SUBLANES, cb), F32) + bdw_ref[:, cs][None]
            for t in range(CONV_WIDTH):
                a, b = divmod(off + t, SUBLANES)
                lo = r0 + a * SUBLANES
                src = ext_ref[lo:lo + rb, cs] if b == 0 else sh_ref[b - 1, lo:lo + rb, :]
                w_t = wdw_ref[t * SUBLANES:(t + 1) * SUBLANES, cs]
                acc = acc + src.reshape(rb // SUBLANES, SUBLANES, cb) * w_t[None]
            y_ref[r0:r0 + rb, cs] = acc.reshape(rb, cb)

    y = y_ref[...]
    mu = jnp.mean(y, axis=-1, keepdims=True)
    yc = y - mu
    yn = yc * lax.rsqrt(jnp.mean(yc * yc, axis=-1, keepdims=True) + EPS) * lng_ref[...] + lnb_ref[...]
    act = _bf(yn * jax.nn.sigmoid(yn))
    o_ref[0] = h_ref[0] + _dot(act, _bf(w2_ref[...]))


def _conv_module(u, w_dw, b_dw, ln_g, ln_b, w2, h, *, ts=256, rb=64, cb=256):
    b, s, c = u.shape
    d = w2.shape[1]
    r = ts // CONV_HALO
    n_sh = ts + CONV_HALO - SUBLANES
    row = lambda i, j: (i, j, 0)
    return pl.pallas_call(
        functools.partial(_conv_kernel, ts=ts, rb=rb, cb=cb),
        out_shape=jax.ShapeDtypeStruct((b, s, d), F32),
        grid=(b, s // ts),
        in_specs=[
            pl.BlockSpec((1, ts, c), row),
            pl.BlockSpec((1, CONV_HALO, c), lambda i, j: (i, jnp.maximum(j * r - 1, 0), 0)),
            _const_spec((CONV_WIDTH * SUBLANES, c)),
            _const_spec((SUBLANES, c)),
            _const_spec((1, c)),
            _const_spec((1, c)),
            _const_spec((c, d)),
            pl.BlockSpec((1, ts, d), row),
        ],
        out_specs=pl.BlockSpec((1, ts, d), row),
        scratch_shapes=[pltpu.VMEM((ts + CONV_HALO, c), F32),
                        pltpu.VMEM((SUBLANES - 1, n_sh, cb), F32),
                        pltpu.VMEM((ts, c), F32)],
        compiler_params=_params(("parallel", "parallel")),
        name="conv_module",
    )(u, u, jnp.repeat(w_dw, SUBLANES, axis=0), jnp.broadcast_to(b_dw, (SUBLANES, c)),
      ln_g.reshape(1, c), ln_b.reshape(1, c), w2, h)


def _ple_kernel(h_ref, p_ref, wp_ref, eg_ref, gg_ref, wg_ref, o_ref):
    h = h_ref[...]
    e = _rms(_dot(_bf(p_ref[...]), _bf(wp_ref[...])), eg_ref[...])
    gate = jax.nn.sigmoid(_dot(_bf(_rms(h, gg_ref[...])), _bf(wg_ref[...])))
    o_ref[...] = h + e * gate


def _ple(h, p, w_proj, e_norm, gate_norm, w_gate, layer, *, tm=512):
    m, d = h.shape
    return pl.pallas_call(
        _ple_kernel,
        out_shape=jax.ShapeDtypeStruct((m, d), F32),
        grid=(m // tm,),
        in_specs=[
            pl.BlockSpec((tm, d), lambda i: (i, 0)),
            pl.BlockSpec((None, tm, D_PLE), lambda i: (layer, i, 0)),
            _const_spec((D_PLE, d), layer),
            _const_spec((1, d), layer),
            _const_spec((1, d), layer),
            _const_spec((d, d), layer),
        ],
        out_specs=pl.BlockSpec((tm, d), lambda i: (i, 0)),
        compiler_params=_params(("parallel",)),
        name="ple",
    )(h, p, w_proj, e_norm.reshape(-1, 1, d), gate_norm.reshape(-1, 1, d), w_gate)


def kernel(x, p, positions, ffn_a_norm, ffn_a_w_in, ffn_a_w_out, ffn_b_norm, ffn_b_w_in, ffn_b_w_out,
           mix_norm, mla_w_in, mla_q_lat_norm, mla_kv_lat_norm, mla_w_uq, mla_w_ukv, mla_q_gain,
           mla_k_gain, mla_w_o, conv_w_pw1, conv_b_pw1, conv_w_dw, conv_b_dw, conv_ln_g, conv_ln_b,
           conv_w_pw2, ple_w_proj, ple_norm, ple_gate_norm, ple_w_gate):
    batch, seq, d = x.shape
    m = batch * seq
    depth = ffn_a_norm.shape[0]
    h = x.reshape(m, d)
    pos = positions.reshape(m)
    p2 = p.reshape(depth, m, -1)
    for i in range(depth):
        h = _ffn(h, ffn_a_norm, ffn_a_w_in, ffn_a_w_out, i)
        j = i // 2
        if i % 2 == 0:
            q, k, v = _mla_proj(h, pos, mix_norm[i], mla_w_in[j], mla_q_lat_norm[j], mla_kv_lat_norm[j],
                                mla_w_uq[j], mla_w_ukv[j], mla_q_gain[j], mla_k_gain[j], batch=batch)
            o = _attention(q, k, v)
            h = _proj_residual(o.reshape(m, N_HEADS * D_V), mla_w_o[j], h)
        else:
            u = _glu(h, mix_norm[i], conv_w_pw1[j], conv_b_pw1[j])
            h = _conv_module(u.reshape(batch, seq, -1), conv_w_dw[j], conv_b_dw[j], conv_ln_g[j],
                             conv_ln_b[j], conv_w_pw2[j], h.reshape(batch, seq, d)).reshape(m, d)
        h = _ffn(h, ffn_b_norm, ffn_b_w_in, ffn_b_w_out, i)
        h = _ple(h, p2, ple_w_proj, ple_norm, ple_gate_norm, ple_w_gate, i)
    return h.reshape(batch, seq, d)
```

```python
import functools
import math

import jax
import jax.numpy as jnp
from jax import lax
from jax.experimental import pallas as pl
from jax.experimental.pallas import tpu as pltpu

D_MODEL = 2048
N_HEADS = 16
Q_LORA = 512
KV_LORA = 512
D_NOPE = 128
D_ROPE = 64
D_V = 128
QK_DIM = D_NOPE + D_ROPE
ROPE_THETA = 10000.0
CONV_WIDTH = 31
D_FF = 5632
D_PLE = 256
EPS = 1e-6
FFN_RESIDUAL_WEIGHT = 0.5

LANES = 128
SUBLANES = 8
V7X_VMEM_LIMIT_BYTES = 60 * 1024 * 1024

QK_PAD = 2 * LANES
CONV_HALO = 32

F32 = jnp.float32
BF16 = jnp.bfloat16


def _bf(x):
    return x.astype(BF16)


def _dot(a, b):
    return jnp.dot(a, b, preferred_element_type=F32)


def _rms(x, g):
    ms = jnp.mean(x * x, axis=-1, keepdims=True)
    return x * lax.rsqrt(ms + EPS) * g


def _params(sem):
    return pltpu.CompilerParams(dimension_semantics=sem,
                                vmem_limit_bytes=V7X_VMEM_LIMIT_BYTES)


def _const_spec(shape, layer=None):
    nd = len(shape)
    if layer is None:
        return pl.BlockSpec(shape, lambda *_: (0,) * nd, pipeline_mode=pl.Buffered(1))
    return pl.BlockSpec((None,) + shape, lambda *_: (layer,) + (0,) * nd,
                        pipeline_mode=pl.Buffered(1))


def _ffn_kernel(x_ref, g_ref, wg_ref, wu_ref, wo_ref, o_ref, xn_ref):
    @pl.when(pl.program_id(1) == 0)
    def _():
        x = x_ref[...]
        xn_ref[...] = _bf(_rms(x, g_ref[...]))
        o_ref[...] = x

    xn = xn_ref[...]
    g = _dot(xn, _bf(wg_ref[...]))
    u = _dot(xn, _bf(wu_ref[...]))
    a = _bf(g * jax.nn.sigmoid(g) * u)
    o_ref[...] += FFN_RESIDUAL_WEIGHT * _dot(a, _bf(wo_ref[...]))


def _ffn(x, norm_g, w_in, w_out, layer, *, tm=1024, tf=256):
    m, d = x.shape
    n_f = D_FF // tf
    return pl.pallas_call(
        _ffn_kernel,
        out_shape=jax.ShapeDtypeStruct((m, d), F32),
        grid=(m // tm, n_f),
        in_specs=[
            pl.BlockSpec((tm, d), lambda i, j: (i, 0)),
            pl.BlockSpec((None, 1, d), lambda i, j: (layer, 0, 0)),
            pl.BlockSpec((None, d, tf), lambda i, j: (layer, 0, j)),
            pl.BlockSpec((None, d, tf), lambda i, j: (layer, 0, j + n_f)),
            pl.BlockSpec((None, tf, d), lambda i, j: (layer, j, 0)),
        ],
        out_specs=pl.BlockSpec((tm, d), lambda i, j: (i, 0)),
        scratch_shapes=[pltpu.VMEM((tm, d), BF16)],
        compiler_params=_params(("parallel", "arbitrary")),
        name="ffn",
    )(x, norm_g.reshape(-1, 1, d), w_in, w_in, w_out)


def _swap_halves_cols(w):
    n = w.shape[-1] // 2
    return jnp.concatenate([w[..., n:], w[..., :n]], axis=-1)


def _row_sum_mxu(sq, ones):
    return _dot(_bf(sq), ones)


def _mla_proj_kernel(x_ref, pos_ref, mixg_ref, win_ref, qlg_ref, kvlg_ref, wuq_ref, wukv_ref,
                     qg_ref, kg_ref, ones_ref, q_ref, k_ref, v_ref):
    xn = _bf(_rms(x_ref[...], mixg_ref[...]))
    lat = _dot(xn, win_ref[...])
    c_q = _bf(_rms(lat[:, :Q_LORA], qlg_ref[...]))
    c_kv = _bf(_rms(lat[:, Q_LORA:Q_LORA + KV_LORA], kvlg_ref[...]))
    k_rope = lat[:, Q_LORA + KV_LORA:]

    tm = x_ref.shape[0]
    half = D_ROPE // 2
    lane = lax.broadcasted_iota(jnp.int32, (tm, LANES), 1)
    inv_freq = jnp.exp((lane % half).astype(F32) * (-2.0 / D_ROPE * math.log(ROPE_THETA)))
    ang = pos_ref[...].astype(F32) * inv_freq
    keep = lane < D_ROPE
    cos = jnp.where(keep, jnp.cos(ang), 0.0)
    sin = jnp.sin(ang)
    sin = jnp.where(keep, jnp.where(lane < half, -sin, sin), 0.0)

    def rope(t):
        return t * cos + pltpu.roll(t, D_ROPE, 1) * sin

    ones_q = ones_ref[...]
    ones_nope = ones_ref[:D_NOPE, :]
    ones_rope = ones_ref[D_NOPE:, :]

    qg = qg_ref[...]
    kg = kg_ref[...]
    k_rot = rope(k_rope * kg[:, D_NOPE:])
    kr_ss = _row_sum_mxu(k_rope * k_rope, ones_rope)
    scale = QK_DIM ** -0.5
    for h in range(N_HEADS):
        cols = slice(h * QK_PAD, (h + 1) * QK_PAD)
        qh = _dot(c_q, wuq_ref[:, cols])
        rq = lax.rsqrt(_row_sum_mxu(qh * qh, ones_q) * (1.0 / QK_DIM) + EPS) * scale
        qh = qh * qg
        q_ref[0, h, :, :D_NOPE] = _bf(qh[:, :D_NOPE] * rq)
        q_ref[0, h, :, D_NOPE:] = _bf(rope(qh[:, D_NOPE:]) * rq)

        kvh = _dot(c_kv, wukv_ref[:, cols])
        kn = kvh[:, :D_NOPE]
        rk = lax.rsqrt((_row_sum_mxu(kn * kn, ones_nope) + kr_ss) * (1.0 / QK_DIM) + EPS)
        k_ref[0, h, :, :D_NOPE] = _bf(kn * kg[:, :D_NOPE] * rk)
        k_ref[0, h, :, D_NOPE:] = _bf(k_rot * rk)
        v_ref[0, h, :, :] = _bf(kvh[:, D_NOPE:])


def _mla_proj(x, pos, mix_g, w_in, q_lat_g, kv_lat_g, w_uq, w_ukv, q_gain, k_gain, *, batch, tm=512):
    m, d = x.shape
    seq = m // batch
    n_s = seq // tm
    lat_w = Q_LORA + KV_LORA + LANES
    w_in_p = _bf(jnp.concatenate([w_in, _swap_halves_cols(w_in[:, Q_LORA + KV_LORA:])], axis=1))
    w_uq_h = w_uq.reshape(Q_LORA, N_HEADS, QK_DIM)
    w_uq_p = _bf(jnp.concatenate([w_uq_h, _swap_halves_cols(w_uq_h[..., D_NOPE:])], axis=-1)
                 .reshape(Q_LORA, N_HEADS * QK_PAD))
    qg_p = jnp.concatenate([q_gain, _swap_halves_cols(q_gain[D_NOPE:])]).reshape(1, QK_PAD)
    kg_p = jnp.concatenate([k_gain, _swap_halves_cols(k_gain[D_NOPE:])]).reshape(1, QK_PAD)
    ones = (lax.broadcasted_iota(jnp.int32, (QK_PAD, LANES), 0) < QK_DIM).astype(BF16)
    hm = lambda w: pl.BlockSpec((1, N_HEADS, tm, w), lambda i: (i // n_s, 0, i % n_s, 0))
    return pl.pallas_call(
        _mla_proj_kernel,
        out_shape=(jax.ShapeDtypeStruct((batch, N_HEADS, seq, QK_PAD), BF16),
                   jax.ShapeDtypeStruct((batch, N_HEADS, seq, QK_PAD), BF16),
                   jax.ShapeDtypeStruct((batch, N_HEADS, seq, D_V), BF16)),
        grid=(m // tm,),
        in_specs=[
            pl.BlockSpec((tm, d), lambda i: (i, 0)),
            pl.BlockSpec((tm, 1), lambda i: (i, 0)),
            _const_spec((1, d)),
            _const_spec((d, lat_w)),
            _const_spec((1, Q_LORA)),
            _const_spec((1, KV_LORA)),
            _const_spec((Q_LORA, N_HEADS * QK_PAD)),
            _const_spec((KV_LORA, N_HEADS * (D_NOPE + D_V))),
            _const_spec((1, QK_PAD)),
            _const_spec((1, QK_PAD)),
            _const_spec((QK_PAD, LANES)),
        ],
        out_specs=(hm(QK_PAD), hm(QK_PAD), hm(D_V)),
        compiler_params=_params(("parallel",)),
        name="mla_proj",
    )(x, pos.reshape(m, 1), mix_g.reshape(1, d), w_in_p, q_lat_g.reshape(1, Q_LORA),
      kv_lat_g.reshape(1, KV_LORA), w_uq_p, _bf(w_ukv), qg_p, kg_p, ones)


def _attn_kernel(q_ref, k_ref, v_ref, o_ref, *, seq, tq, hb):
    def scores(q, k):
        return lax.dot_general(q, k, (((1,), (1,)), ((), ())), preferred_element_type=F32)

    row = lax.broadcasted_iota(jnp.int32, (tq, tq), 0)
    col = lax.broadcasted_iota(jnp.int32, (tq, tq), 1)
    causal = col <= row
    for hh in range(hb):
        for qi in range(seq // tq):
            lo = qi * tq
            q = q_ref[0, hh, lo:lo + tq, :]
            s_d = jnp.where(causal, scores(q, k_ref[0, hh, lo:lo + tq, :]), -jnp.inf)
            m = jnp.max(s_d, axis=-1, keepdims=True)
            if qi:
                s_o = scores(q, k_ref[0, hh, :lo, :])
                m = jnp.maximum(m, jnp.max(s_o, axis=-1, keepdims=True))
            p_d = jnp.exp(s_d - m)
            l = jnp.sum(p_d, axis=-1, keepdims=True)
            o = _dot(_bf(p_d), v_ref[0, hh, lo:lo + tq, :])
            if qi:
                p_o = jnp.exp(s_o - m)
                l = l + jnp.sum(p_o, axis=-1, keepdims=True)
                o = o + _dot(_bf(p_o), v_ref[0, hh, :lo, :])
            o_ref[0, lo:lo + tq, hh * D_V:(hh + 1) * D_V] = _bf(o / l)


def _attention(q, k, v, *, tq=512, hb=4):
    b, h, s, _ = q.shape
    return pl.pallas_call(
        functools.partial(_attn_kernel, seq=s, tq=tq, hb=hb),
        out_shape=jax.ShapeDtypeStruct((b, s, h * D_V), BF16),
        grid=(b, h // hb),
        in_specs=[
            pl.BlockSpec((1, hb, s, QK_PAD), lambda i, j: (i, j, 0, 0)),
            pl.BlockSpec((1, hb, s, QK_PAD), lambda i, j: (i, j, 0, 0)),
            pl.BlockSpec((1, hb, s, D_V), lambda i, j: (i, j, 0, 0)),
        ],
        out_specs=pl.BlockSpec((1, s, hb * D_V), lambda i, j: (i, 0, j)),
        compiler_params=_params(("parallel", "parallel")),
        name="attention",
    )(q, k, v)


def _proj_res_kernel(a_ref, w_ref, h_ref, o_ref):
    o_ref[...] = h_ref[...] + _dot(a_ref[...], _bf(w_ref[...]))


def _proj_residual(a, w, h, *, tm=512):
    m, d = h.shape
    k = a.shape[1]
    return pl.pallas_call(
        _proj_res_kernel,
        out_shape=jax.ShapeDtypeStruct((m, d), F32),
        grid=(m // tm,),
        in_specs=[
            pl.BlockSpec((tm, k), lambda i: (i, 0)),
            _const_spec((k, d)),
            pl.BlockSpec((tm, d), lambda i: (i, 0)),
        ],
        out_specs=pl.BlockSpec((tm, d), lambda i: (i, 0)),
        compiler_params=_params(("parallel",)),
        name="attn_out_proj",
    )(a, w, h)


def _glu_kernel(x_ref, g_ref, wa_ref, wg_ref, ba_ref, bg_ref, u_ref, xn_ref):
    @pl.when(pl.program_id(1) == 0)
    def _():
        xn_ref[...] = _bf(_rms(x_ref[...], g_ref[...]))

    xn = xn_ref[...]
    a = _dot(xn, _bf(wa_ref[...])) + ba_ref[...]
    g = _dot(xn, _bf(wg_ref[...])) + bg_ref[...]
    u_ref[...] = a * jax.nn.sigmoid(g)


def _glu(x, norm_g, w, b, *, tm=1024, tn=512):
    m, d = x.shape
    c = w.shape[1] // 2
    n_c = c // tn
    b2 = b.reshape(1, 2 * c)
    return pl.pallas_call(
        _glu_kernel,
        out_shape=jax.ShapeDtypeStruct((m, c), F32),
        grid=(m // tm, n_c),
        in_specs=[
            pl.BlockSpec((tm, d), lambda i, j: (i, 0)),
            pl.BlockSpec((1, d), lambda i, j: (0, 0)),
            pl.BlockSpec((d, tn), lambda i, j: (0, j)),
            pl.BlockSpec((d, tn), lambda i, j: (0, j + n_c)),
            pl.BlockSpec((1, tn), lambda i, j: (0, j)),
            pl.BlockSpec((1, tn), lambda i, j: (0, j + n_c)),
        ],
        out_specs=pl.BlockSpec((tm, tn), lambda i, j: (i, j)),
        scratch_shapes=[pltpu.VMEM((tm, d), BF16)],
        compiler_params=_params(("parallel", "arbitrary")),
        name="conv_glu",
    )(x, norm_g.reshape(1, d), w, w, b2, b2)


def _conv_kernel(u_ref, halo_ref, wdw_ref, bdw_ref, lng_ref, lnb_ref, w2_ref, h_ref, o_ref,
                 ext_ref, sh_ref, y_ref, *, ts, rb, cb):
    i = pl.program_id(1)
    halo = halo_ref[0]
    ext_ref[:CONV_HALO, :] = jnp.where(i > 0, halo, jnp.zeros_like(halo))
    ext_ref[CONV_HALO:, :] = u_ref[0]

    off = CONV_HALO - (CONV_WIDTH - 1)
    n_sh = sh_ref.shape[1]
    c = u_ref.shape[2]
    for c0 in range(0, c, cb):
        cs = slice(c0, c0 + cb)
        for b in range(1, SUBLANES):
            sh_ref[b - 1] = ext_ref[b:b + n_sh, cs]
        for r0 in range(0, ts, rb):
            acc = jnp.zeros((rb // SUBLANES, SUBLANES, cb), F32) + bdw_ref[:, cs][None]
            for t in range(CONV_WIDTH):
                a, b = divmod(off + t, SUBLANES)
                lo = r0 + a * SUBLANES
                src = ext_ref[lo:lo + rb, cs] if b == 0 else sh_ref[b - 1, lo:lo + rb, :]
                w_t = wdw_ref[t * SUBLANES:(t + 1) * SUBLANES, cs]
                acc = acc + src.reshape(rb // SUBLANES, SUBLANES, cb) * w_t[None]
            y_ref[r0:r0 + rb, cs] = acc.reshape(rb, cb)

    y = y_ref[...]
    mu = jnp.mean(y, axis=-1, keepdims=True)
    yc = y - mu
    yn = yc * lax.rsqrt(jnp.mean(yc * yc, axis=-1, keepdims=True) + EPS) * lng_ref[...] + lnb_ref[...]
    act = _bf(yn * jax.nn.sigmoid(yn))
    o_ref[0] = h_ref[0] + _dot(act, _bf(w2_ref[...]))


def _conv_module(u, w_dw, b_dw, ln_g, ln_b, w2, h, *, ts=256, rb=64, cb=256):
    b, s, c = u.shape
    d = w2.shape[1]
    r = ts // CONV_HALO
    n_sh = ts + CONV_HALO - SUBLANES
    row = lambda i, j: (i, j, 0)
    return pl.pallas_call(
        functools.partial(_conv_kernel, ts=ts, rb=rb, cb=cb),
        out_shape=jax.ShapeDtypeStruct((b, s, d), F32),
        grid=(b, s // ts),
        in_specs=[
            pl.BlockSpec((1, ts, c), row),
            pl.BlockSpec((1, CONV_HALO, c), lambda i, j: (i, jnp.maximum(j * r - 1, 0), 0)),
            _const_spec((CONV_WIDTH * SUBLANES, c)),
            _const_spec((SUBLANES, c)),
            _const_spec((1, c)),
            _const_spec((1, c)),
            _const_spec((c, d)),
            pl.BlockSpec((1, ts, d), row),
        ],
        out_specs=pl.BlockSpec((1, ts, d), row),
        scratch_shapes=[pltpu.VMEM((ts + CONV_HALO, c), F32),
                        pltpu.VMEM((SUBLANES - 1, n_sh, cb), F32),
                        pltpu.VMEM((ts, c), F32)],
        compiler_params=_params(("parallel", "parallel")),
        name="conv_module",
    )(u, u, jnp.repeat(w_dw, SUBLANES, axis=0), jnp.broadcast_to(b_dw, (SUBLANES, c)),
      ln_g.reshape(1, c), ln_b.reshape(1, c), w2, h)


def _ple_kernel(h_ref, p_ref, wp_ref, eg_ref, gg_ref, wg_ref, o_ref):
    h = h_ref[...]
    e = _rms(_dot(_bf(p_ref[...]), _bf(wp_ref[...])), eg_ref[...])
    gate = jax.nn.sigmoid(_dot(_bf(_rms(h, gg_ref[...])), _bf(wg_ref[...])))
    o_ref[...] = h + e * gate


def _ple(h, p, w_proj, e_norm, gate_norm, w_gate, layer, *, tm=512):
    m, d = h.shape
    return pl.pallas_call(
        _ple_kernel,
        out_shape=jax.ShapeDtypeStruct((m, d), F32),
        grid=(m // tm,),
        in_specs=[
            pl.BlockSpec((tm, d), lambda i: (i, 0)),
            pl.BlockSpec((None, tm, D_PLE), lambda i: (layer, i, 0)),
            _const_spec((D_PLE, d), layer),
            _const_spec((1, d), layer),
            _const_spec((1, d), layer),
            _const_spec((d, d), layer),
        ],
        out_specs=pl.BlockSpec((tm, d), lambda i: (i, 0)),
        compiler_params=_params(("parallel",)),
        name="ple",
    )(h, p, w_proj, e_norm.reshape(-1, 1, d), gate_norm.reshape(-1, 1, d), w_gate)


def kernel(x, p, positions, ffn_a_norm, ffn_a_w_in, ffn_a_w_out, ffn_b_norm, ffn_b_w_in, ffn_b_w_out,
           mix_norm, mla_w_in, mla_q_lat_norm, mla_kv_lat_norm, mla_w_uq, mla_w_ukv, mla_q_gain,
           mla_k_gain, mla_w_o, conv_w_pw1, conv_b_pw1, conv_w_dw, conv_b_dw, conv_ln_g, conv_ln_b,
           conv_w_pw2, ple_w_proj, ple_norm, ple_gate_norm, ple_w_gate):
    batch, seq, d = x.shape
    m = batch * seq
    depth = ffn_a_norm.shape[0]
    h = x.reshape(m, d)
    pos = positions.reshape(m)
    p2 = p.reshape(depth, m, -1)
    for i in range(depth):
        h = _ffn(h, ffn_a_norm, ffn_a_w_in, ffn_a_w_out, i)
        j = i // 2
        if i % 2 == 0:
            q, k, v = _mla_proj(h, pos, mix_norm[i], mla_w_in[j], mla_q_lat_norm[j], mla_kv_lat_norm[j],
                                mla_w_uq[j], mla_w_ukv[j], mla_q_gain[j], mla_k_gain[j], batch=batch)
            o = _attention(q, k, v)
            h = _proj_residual(o.reshape(m, N_HEADS * D_V), mla_w_o[j], h)
        else:
            u = _glu(h, mix_norm[i], conv_w_pw1[j], conv_b_pw1[j])
            h = _conv_module(u.reshape(batch, seq, -1), conv_w_dw[j], conv_b_dw[j], conv_ln_g[j],
                             conv_ln_b[j], conv_w_pw2[j], h.reshape(batch, seq, d)).reshape(m, d)
        h = _ffn(h, ffn_b_norm, ffn_b_w_in, ffn_b_w_out, i)
        h = _ple(h, p2, ple_w_proj, ple_norm, ple_gate_norm, ple_w_gate, i)
    return h.reshape(batch, seq, d)
```

```python
import functools
import math

import jax
import jax.numpy as jnp
import numpy as np
from jax import lax
from jax.experimental import pallas as pl
from jax.experimental.pallas import tpu as pltpu

D_MODEL = 2048
N_HEADS = 16
Q_LORA = 512
KV_LORA = 512
D_NOPE = 128
D_ROPE = 64
D_V = 128
QK_DIM = D_NOPE + D_ROPE
ROPE_THETA = 10000.0
CONV_WIDTH = 31
D_FF = 5632
D_PLE = 256
EPS = 1e-6
FFN_RESIDUAL_WEIGHT = 0.5

LANES = 128
V7X_VMEM_LIMIT_BYTES = 60 * 1024 * 1024

QK_PAD = 2 * LANES
CONV_HIST = LANES
CONV_TAPS_P = 32

F32 = jnp.float32
BF16 = jnp.bfloat16


def _bf(x):
    return x.astype(BF16)


def _dot(a, b):
    return jnp.dot(a, b, preferred_element_type=F32)


def _rms(x, g):
    ms = jnp.mean(x * x, axis=-1, keepdims=True)
    return x * lax.rsqrt(ms + EPS) * g


def _params(sem):
    return pltpu.CompilerParams(dimension_semantics=sem,
                                vmem_limit_bytes=V7X_VMEM_LIMIT_BYTES)


def _const_spec(shape, layer=None):
    nd = len(shape)
    if layer is None:
        return pl.BlockSpec(shape, lambda *_: (0,) * nd, pipeline_mode=pl.Buffered(1))
    return pl.BlockSpec((None,) + shape, lambda *_: (layer,) + (0,) * nd,
                        pipeline_mode=pl.Buffered(1))


def _ffn_kernel(x_ref, g_ref, wg_ref, wu_ref, wo_ref, o_ref, xn_ref):
    @pl.when(pl.program_id(1) == 0)
    def _():
        x = x_ref[...]
        xn_ref[...] = _bf(_rms(x, g_ref[...]))
        o_ref[...] = x

    xn = xn_ref[...]
    g = _dot(xn, _bf(wg_ref[...]))
    u = _dot(xn, _bf(wu_ref[...]))
    a = _bf(g * jax.nn.sigmoid(g) * u)
    o_ref[...] += FFN_RESIDUAL_WEIGHT * _dot(a, _bf(wo_ref[...]))


def _ffn(x, norm_g, w_in, w_out, layer, *, tm=1024, tf=256):
    m, d = x.shape
    n_f = D_FF // tf
    return pl.pallas_call(
        _ffn_kernel,
        out_shape=jax.ShapeDtypeStruct((m, d), F32),
        grid=(m // tm, n_f),
        in_specs=[
            pl.BlockSpec((tm, d), lambda i, j: (i, 0)),
            pl.BlockSpec((None, 1, d), lambda i, j: (layer, 0, 0)),
            pl.BlockSpec((None, d, tf), lambda i, j: (layer, 0, j)),
            pl.BlockSpec((None, d, tf), lambda i, j: (layer, 0, j + n_f)),
            pl.BlockSpec((None, tf, d), lambda i, j: (layer, j, 0)),
        ],
        out_specs=pl.BlockSpec((tm, d), lambda i, j: (i, 0)),
        scratch_shapes=[pltpu.VMEM((tm, d), BF16)],
        compiler_params=_params(("parallel", "arbitrary")),
        name="ffn",
    )(x, norm_g.reshape(-1, 1, d), w_in, w_in, w_out)


def _swap_halves_cols(w):
    n = w.shape[-1] // 2
    return jnp.concatenate([w[..., n:], w[..., :n]], axis=-1)


def _row_sum_mxu(sq, ones):
    return _dot(_bf(sq), ones)


def _mla_proj_kernel(x_ref, pos_ref, mixg_ref, win_ref, qlg_ref, kvlg_ref, wuq_ref, wukv_ref,
                     qg_ref, kg_ref, ones_ref, q_ref, k_ref, v_ref):
    xn = _bf(_rms(x_ref[...], mixg_ref[...]))
    lat = _dot(xn, win_ref[...])
    c_q = _bf(_rms(lat[:, :Q_LORA], qlg_ref[...]))
    c_kv = _bf(_rms(lat[:, Q_LORA:Q_LORA + KV_LORA], kvlg_ref[...]))
    k_rope = lat[:, Q_LORA + KV_LORA:]

    tm = x_ref.shape[0]
    half = D_ROPE // 2
    lane = lax.broadcasted_iota(jnp.int32, (tm, LANES), 1)
    inv_freq = jnp.exp((lane % half).astype(F32) * (-2.0 / D_ROPE * math.log(ROPE_THETA)))
    ang = pos_ref[...].astype(F32) * inv_freq
    keep = lane < D_ROPE
    cos = jnp.where(keep, jnp.cos(ang), 0.0)
    sin = jnp.sin(ang)
    sin = jnp.where(keep, jnp.where(lane < half, -sin, sin), 0.0)

    def rope(t):
        return t * cos + pltpu.roll(t, D_ROPE, 1) * sin

    ones_q = ones_ref[...]
    ones_nope = ones_ref[:D_NOPE, :]
    ones_rope = ones_ref[D_NOPE:, :]

    qg = qg_ref[...]
    kg = kg_ref[...]
    k_rot = rope(k_rope * kg[:, D_NOPE:])
    kr_ss = _row_sum_mxu(k_rope * k_rope, ones_rope)
    scale = QK_DIM ** -0.5
    for h in range(N_HEADS):
        cols = slice(h * QK_PAD, (h + 1) * QK_PAD)
        qh = _dot(c_q, wuq_ref[:, cols])
        rq = lax.rsqrt(_row_sum_mxu(qh * qh, ones_q) * (1.0 / QK_DIM) + EPS) * scale
        qh = qh * qg
        q_ref[0, h, :, :D_NOPE] = _bf(qh[:, :D_NOPE] * rq)
        q_ref[0, h, :, D_NOPE:] = _bf(rope(qh[:, D_NOPE:]) * rq)

        kvh = _dot(c_kv, wukv_ref[:, cols])
        kn = kvh[:, :D_NOPE]
        rk = lax.rsqrt((_row_sum_mxu(kn * kn, ones_nope) + kr_ss) * (1.0 / QK_DIM) + EPS)
        k_ref[0, h, :, :D_NOPE] = _bf(kn * kg[:, :D_NOPE] * rk)
        k_ref[0, h, :, D_NOPE:] = _bf(k_rot * rk)
        v_ref[0, h, :, :] = _bf(kvh[:, D_NOPE:])


def _mla_proj(x, pos, mix_g, w_in, q_lat_g, kv_lat_g, w_uq, w_ukv, q_gain, k_gain, *, batch, tm=512):
    m, d = x.shape
    seq = m // batch
    n_s = seq // tm
    lat_w = Q_LORA + KV_LORA + LANES
    w_in_p = _bf(jnp.concatenate([w_in, _swap_halves_cols(w_in[:, Q_LORA + KV_LORA:])], axis=1))
    w_uq_h = w_uq.reshape(Q_LORA, N_HEADS, QK_DIM)
    w_uq_p = _bf(jnp.concatenate([w_uq_h, _swap_halves_cols(w_uq_h[..., D_NOPE:])], axis=-1)
                 .reshape(Q_LORA, N_HEADS * QK_PAD))
    qg_p = jnp.concatenate([q_gain, _swap_halves_cols(q_gain[D_NOPE:])]).reshape(1, QK_PAD)
    kg_p = jnp.concatenate([k_gain, _swap_halves_cols(k_gain[D_NOPE:])]).reshape(1, QK_PAD)
    ones = (lax.broadcasted_iota(jnp.int32, (QK_PAD, LANES), 0) < QK_DIM).astype(BF16)
    hm = lambda w: pl.BlockSpec((1, N_HEADS, tm, w), lambda i: (i // n_s, 0, i % n_s, 0))
    return pl.pallas_call(
        _mla_proj_kernel,
        out_shape=(jax.ShapeDtypeStruct((batch, N_HEADS, seq, QK_PAD), BF16),
                   jax.ShapeDtypeStruct((batch, N_HEADS, seq, QK_PAD), BF16),
                   jax.ShapeDtypeStruct((batch, N_HEADS, seq, D_V), BF16)),
        grid=(m // tm,),
        in_specs=[
            pl.BlockSpec((tm, d), lambda i: (i, 0)),
            pl.BlockSpec((tm, 1), lambda i: (i, 0)),
            _const_spec((1, d)),
            _const_spec((d, lat_w)),
            _const_spec((1, Q_LORA)),
            _const_spec((1, KV_LORA)),
            _const_spec((Q_LORA, N_HEADS * QK_PAD)),
            _const_spec((KV_LORA, N_HEADS * (D_NOPE + D_V))),
            _const_spec((1, QK_PAD)),
            _const_spec((1, QK_PAD)),
            _const_spec((QK_PAD, LANES)),
        ],
        out_specs=(hm(QK_PAD), hm(QK_PAD), hm(D_V)),
        compiler_params=_params(("parallel",)),
        name="mla_proj",
    )(x, pos.reshape(m, 1), mix_g.reshape(1, d), w_in_p, q_lat_g.reshape(1, Q_LORA),
      kv_lat_g.reshape(1, KV_LORA), w_uq_p, _bf(w_ukv), qg_p, kg_p, ones)


def _attn_kernel(q_ref, k_ref, v_ref, o_ref, *, seq, tq, hb):
    def scores(q, k):
        return lax.dot_general(q, k, (((1,), (1,)), ((), ())), preferred_element_type=F32)

    row = lax.broadcasted_iota(jnp.int32, (tq, tq), 0)
    col = lax.broadcasted_iota(jnp.int32, (tq, tq), 1)
    causal = col <= row
    for hh in range(hb):
        for qi in range(seq // tq):
            lo = qi * tq
            q = q_ref[0, hh, lo:lo + tq, :]
            s_d = jnp.where(causal, scores(q, k_ref[0, hh, lo:lo + tq, :]), -jnp.inf)
            m = jnp.max(s_d, axis=-1, keepdims=True)
            if qi:
                s_o = scores(q, k_ref[0, hh, :lo, :])
                m = jnp.maximum(m, jnp.max(s_o, axis=-1, keepdims=True))
            p_d = jnp.exp(s_d - m)
            l = jnp.sum(p_d, axis=-1, keepdims=True)
            o = _dot(_bf(p_d), v_ref[0, hh, lo:lo + tq, :])
            if qi:
                p_o = jnp.exp(s_o - m)
                l = l + jnp.sum(p_o, axis=-1, keepdims=True)
                o = o + _dot(_bf(p_o), v_ref[0, hh, :lo, :])
            o_ref[0, lo:lo + tq, hh * D_V:(hh + 1) * D_V] = _bf(o / l)


def _attention(q, k, v, *, tq=512, hb=4):
    b, h, s, _ = q.shape
    return pl.pallas_call(
        functools.partial(_attn_kernel, seq=s, tq=tq, hb=hb),
        out_shape=jax.ShapeDtypeStruct((b, s, h * D_V), BF16),
        grid=(b, h // hb),
        in_specs=[
            pl.BlockSpec((1, hb, s, QK_PAD), lambda i, j: (i, j, 0, 0)),
            pl.BlockSpec((1, hb, s, QK_PAD), lambda i, j: (i, j, 0, 0)),
            pl.BlockSpec((1, hb, s, D_V), lambda i, j: (i, j, 0, 0)),
        ],
        out_specs=pl.BlockSpec((1, s, hb * D_V), lambda i, j: (i, 0, j)),
        compiler_params=_params(("parallel", "parallel")),
        name="attention",
    )(q, k, v)


def _proj_res_kernel(a_ref, w_ref, h_ref, o_ref):
    o_ref[...] = h_ref[...] + _dot(a_ref[...], _bf(w_ref[...]))


def _proj_residual(a, w, h, *, tm=512):
    m, d = h.shape
    k = a.shape[1]
    return pl.pallas_call(
        _proj_res_kernel,
        out_shape=jax.ShapeDtypeStruct((m, d), F32),
        grid=(m // tm,),
        in_specs=[
            pl.BlockSpec((tm, k), lambda i: (i, 0)),
            _const_spec((k, d)),
            pl.BlockSpec((tm, d), lambda i: (i, 0)),
        ],
        out_specs=pl.BlockSpec((tm, d), lambda i: (i, 0)),
        compiler_params=_params(("parallel",)),
        name="attn_out_proj",
    )(a, w, h)


def _glu_kernel(x_ref, g_ref, wa_ref, wg_ref, ba_ref, bg_ref, u_ref, xn_ref):
    @pl.when(pl.program_id(1) == 0)
    def _():
        xn_ref[...] = _bf(_rms(x_ref[...], g_ref[...]))

    xn = xn_ref[...]
    a = _dot(xn, _bf(wa_ref[...])) + ba_ref[...]
    g = _dot(xn, _bf(wg_ref[...])) + bg_ref[...]
    u_ref[...] = _bf(a * jax.nn.sigmoid(g))


def _glu(x, norm_g, w, b, *, tm=1024, tn=512):
    m, d = x.shape
    c = w.shape[1] // 2
    n_c = c // tn
    b2 = b.reshape(1, 2 * c)
    return pl.pallas_call(
        _glu_kernel,
        out_shape=jax.ShapeDtypeStruct((m, c), BF16),
        grid=(m // tm, n_c),
        in_specs=[
            pl.BlockSpec((tm, d), lambda i, j: (i, 0)),
            pl.BlockSpec((1, d), lambda i, j: (0, 0)),
            pl.BlockSpec((d, tn), lambda i, j: (0, j)),
            pl.BlockSpec((d, tn), lambda i, j: (0, j + n_c)),
            pl.BlockSpec((1, tn), lambda i, j: (0, j)),
            pl.BlockSpec((1, tn), lambda i, j: (0, j + n_c)),
        ],
        out_specs=pl.BlockSpec((tm, tn), lambda i, j: (i, j)),
        scratch_shapes=[pltpu.VMEM((tm, d), BF16)],
        compiler_params=_params(("parallel", "arbitrary")),
        name="conv_glu",
    )(x, norm_g.reshape(1, d), w, w, b2, b2)


def _dft_tables(ts):
    n = CONV_HIST + ts
    n_bins = n // 2 + 1
    n_bins_p = -(-n_bins // LANES) * LANES
    f = np.arange(n_bins_p, dtype=np.float64)[:, None]
    valid = (f < n_bins).astype(np.float64)
    r = np.arange(n, dtype=np.float64)[None, :]
    ang = 2.0 * np.pi * f * r / n
    fwd = np.concatenate([np.cos(ang) * valid, -np.sin(ang) * valid], axis=0)
    weight = np.where((f == 0) | (f == n // 2), 1.0, 2.0) * valid / n
    ang_o = ang[:, CONV_HIST:]
    inv = np.concatenate([(np.cos(ang_o) * weight).T, (-np.sin(ang_o) * weight).T], axis=1)
    delay = (CONV_WIDTH - 1) - np.arange(CONV_TAPS_P, dtype=np.float64)[None, :]
    tap_ok = (np.arange(CONV_TAPS_P) < CONV_WIDTH).astype(np.float64)[None, :]
    ang_t = 2.0 * np.pi * f * delay / n
    taps = np.concatenate([np.cos(ang_t) * valid * tap_ok, -np.sin(ang_t) * valid * tap_ok], axis=0)
    return fwd, inv, taps


def _conv_kernel(u_ref, hist_ref, fwd_ref, inv_ref, taps_ref, wdw_ref, bdw_ref, lng_ref, lnb_ref,
                 w2_ref, h_ref, o_ref, spec_ref, y_ref, *, cn):
    @pl.when((pl.program_id(0) == 0) & (pl.program_id(1) == 0))
    def _():
        spec_ref[...] = jnp.dot(taps_ref[...], wdw_ref[...], precision=lax.Precision.HIGHEST,
                                preferred_element_type=F32)

    n_bins_p = fwd_ref.shape[0] // 2
    have_hist = pl.program_id(1) > 0
    c = u_ref.shape[2]
    for c0 in range(0, c, cn):
        cols = slice(c0, c0 + cn)
        hist = hist_ref[0, :, cols]
        x = jnp.concatenate([jnp.where(have_hist, hist, jnp.zeros_like(hist)),
                             u_ref[0, :, cols]], axis=0)
        xs = _dot(fwd_ref[...], x)
        xr, xi = xs[:n_bins_p], xs[n_bins_p:]
        fr, fi = spec_ref[:n_bins_p, cols], spec_ref[n_bins_p:, cols]
        ys = jnp.concatenate([_bf(xr * fr - xi * fi), _bf(xr * fi + xi * fr)], axis=0)
        y_ref[:, cols] = _dot(inv_ref[...], ys) + bdw_ref[:, cols]

    y = y_ref[...]
    mu = jnp.mean(y, axis=-1, keepdims=True)
    yc = y - mu
    yn = yc * lax.rsqrt(jnp.mean(yc * yc, axis=-1, keepdims=True) + EPS) * lng_ref[...] + lnb_ref[...]
    act = _bf(yn * jax.nn.sigmoid(yn))
    o_ref[0] = h_ref[0] + _dot(act, _bf(w2_ref[...]))


def _conv_module(u, w_dw, b_dw, ln_g, ln_b, w2, h, *, ts=256, cn=512):
    b, s, c = u.shape
    d = w2.shape[1]
    r = ts // CONV_HIST
    fwd, inv, taps = _dft_tables(ts)
    n_spec = fwd.shape[0]
    row = lambda i, j: (i, j, 0)
    return pl.pallas_call(
        functools.partial(_conv_kernel, cn=cn),
        out_shape=jax.ShapeDtypeStruct((b, s, d), F32),
        grid=(b, s // ts),
        in_specs=[
            pl.BlockSpec((1, ts, c), row),
            pl.BlockSpec((1, CONV_HIST, c), lambda i, j: (i, jnp.maximum(j * r - 1, 0), 0)),
            _const_spec(fwd.shape),
            _const_spec(inv.shape),
            _const_spec(taps.shape),
            _const_spec((CONV_TAPS_P, c)),
            _const_spec((1, c)),
            _const_spec((1, c)),
            _const_spec((1, c)),
            _const_spec((c, d)),
            pl.BlockSpec((1, ts, d), row),
        ],
        out_specs=pl.BlockSpec((1, ts, d), row),
        scratch_shapes=[pltpu.VMEM((n_spec, c), F32), pltpu.VMEM((ts, c), F32)],
        compiler_params=_params(("arbitrary", "arbitrary")),
        name="conv_module",
    )(u, u, jnp.asarray(fwd, BF16), jnp.asarray(inv, BF16), jnp.asarray(taps, F32),
      jnp.pad(w_dw, ((0, CONV_TAPS_P - CONV_WIDTH), (0, 0))), b_dw.reshape(1, c),
      ln_g.reshape(1, c), ln_b.reshape(1, c), w2, h)


def _ple_kernel(h_ref, p_ref, wp_ref, eg_ref, gg_ref, wg_ref, o_ref):
    h = h_ref[...]
    e = _rms(_dot(_bf(p_ref[...]), _bf(wp_ref[...])), eg_ref[...])
    gate = jax.nn.sigmoid(_dot(_bf(_rms(h, gg_ref[...])), _bf(wg_ref[...])))
    o_ref[...] = h + e * gate


def _ple(h, p, w_proj, e_norm, gate_norm, w_gate, layer, *, tm=512):
    m, d = h.shape
    return pl.pallas_call(
        _ple_kernel,
        out_shape=jax.ShapeDtypeStruct((m, d), F32),
        grid=(m // tm,),
        in_specs=[
            pl.BlockSpec((tm, d), lambda i: (i, 0)),
            pl.BlockSpec((None, tm, D_PLE), lambda i: (layer, i, 0)),
            _const_spec((D_PLE, d), layer),
            _const_spec((1, d), layer),
            _const_spec((1, d), layer),
            _const_spec((d, d), layer),
        ],
        out_specs=pl.BlockSpec((tm, d), lambda i: (i, 0)),
        compiler_params=_params(("parallel",)),
        name="ple",
    )(h, p, w_proj, e_norm.reshape(-1, 1, d), gate_norm.reshape(-1, 1, d), w_gate)


def kernel(x, p, positions, ffn_a_norm, ffn_a_w_in, ffn_a_w_out, ffn_b_norm, ffn_b_w_in, ffn_b_w_out,
           mix_norm, mla_w_in, mla_q_lat_norm, mla_kv_lat_norm, mla_w_uq, mla_w_ukv, mla_q_gain,
           mla_k_gain, mla_w_o, conv_w_pw1, conv_b_pw1, conv_w_dw, conv_b_dw, conv_ln_g, conv_ln_b,
           conv_w_pw2, ple_w_proj, ple_norm, ple_gate_norm, ple_w_gate):
    batch, seq, d = x.shape
    m = batch * seq
    depth = ffn_a_norm.shape[0]
    h = x.reshape(m, d)
    pos = positions.reshape(m)
    p2 = p.reshape(depth, m, -1)
    for i in range(depth):
        h = _ffn(h, ffn_a_norm, ffn_a_w_in, ffn_a_w_out, i)
        j = i // 2
        if i % 2 == 0:
            q, k, v = _mla_proj(h, pos, mix_norm[i], mla_w_in[j], mla_q_lat_norm[j], mla_kv_lat_norm[j],
                                mla_w_uq[j], mla_w_ukv[j], mla_q_gain[j], mla_k_gain[j], batch=batch)
            o = _attention(q, k, v)
            h = _proj_residual(o.reshape(m, N_HEADS * D_V), mla_w_o[j], h)
        else:
            u = _glu(h, mix_norm[i], conv_w_pw1[j], conv_b_pw1[j])
            h = _conv_module(u.reshape(batch, seq, -1), conv_w_dw[j], conv_b_dw[j], conv_ln_g[j],
                             conv_ln_b[j], conv_w_pw2[j], h.reshape(batch, seq, d)).reshape(m, d)
        h = _ffn(h, ffn_b_norm, ffn_b_w_in, ffn_b_w_out, i)
        h = _ple(h, p2, ple_w_proj, ple_norm, ple_gate_norm, ple_w_gate, i)
    return h.reshape(batch, seq, d)
```

```python
import functools
import math

import jax
import jax.numpy as jnp
import numpy as np
from jax import lax
from jax.experimental import pallas as pl
from jax.experimental.pallas import tpu as pltpu

D_MODEL = 2048
N_HEADS = 16
Q_LORA = 512
KV_LORA = 512
D_NOPE = 128
D_ROPE = 64
D_V = 128
QK_DIM = D_NOPE + D_ROPE
ROPE_THETA = 10000.0
CONV_WIDTH = 31
D_FF = 5632
D_PLE = 256
EPS = 1e-6
FFN_RESIDUAL_WEIGHT = 0.5

LANES = 128
V7X_VMEM_LIMIT_BYTES = 60 * 1024 * 1024

QK_PAD = 2 * LANES
CONV_HIST = LANES
CONV_TAPS_P = 32

F32 = jnp.float32
BF16 = jnp.bfloat16


def _bf(x):
    return x.astype(BF16)


def _dot(a, b):
    return jnp.dot(a, b, preferred_element_type=F32)


def _rms(x, g):
    ms = jnp.mean(x * x, axis=-1, keepdims=True)
    return x * lax.rsqrt(ms + EPS) * g


def _params(sem):
    return pltpu.CompilerParams(dimension_semantics=sem,
                                vmem_limit_bytes=V7X_VMEM_LIMIT_BYTES)


def _const_spec(shape, layer=None):
    nd = len(shape)
    if layer is None:
        return pl.BlockSpec(shape, lambda *_: (0,) * nd, pipeline_mode=pl.Buffered(1))
    return pl.BlockSpec((None,) + shape, lambda *_: (layer,) + (0,) * nd,
                        pipeline_mode=pl.Buffered(1))


def _ffn_kernel(x_ref, g_ref, wg_ref, wu_ref, wo_ref, o_ref, xn_ref):
    @pl.when(pl.program_id(1) == 0)
    def _():
        x = x_ref[...]
        xn_ref[...] = _bf(_rms(x, g_ref[...]))
        o_ref[...] = x

    xn = xn_ref[...]
    g = _dot(xn, _bf(wg_ref[...]))
    u = _dot(xn, _bf(wu_ref[...]))
    a = _bf(g * jax.nn.sigmoid(g) * u)
    o_ref[...] += FFN_RESIDUAL_WEIGHT * _dot(a, _bf(wo_ref[...]))


def _ffn(x, norm_g, w_in, w_out, layer, *, tm=1024, tf=256):
    m, d = x.shape
    n_f = D_FF // tf
    return pl.pallas_call(
        _ffn_kernel,
        out_shape=jax.ShapeDtypeStruct((m, d), F32),
        grid=(m // tm, n_f),
        in_specs=[
            pl.BlockSpec((tm, d), lambda i, j: (i, 0)),
            pl.BlockSpec((None, 1, d), lambda i, j: (layer, 0, 0)),
            pl.BlockSpec((None, d, tf), lambda i, j: (layer, 0, j)),
            pl.BlockSpec((None, d, tf), lambda i, j: (layer, 0, j + n_f)),
            pl.BlockSpec((None, tf, d), lambda i, j: (layer, j, 0)),
        ],
        out_specs=pl.BlockSpec((tm, d), lambda i, j: (i, 0)),
        scratch_shapes=[pltpu.VMEM((tm, d), BF16)],
        compiler_params=_params(("parallel", "arbitrary")),
        name="ffn",
    )(x, norm_g.reshape(-1, 1, d), w_in, w_in, w_out)


def _swap_halves_cols(w):
    n = w.shape[-1] // 2
    return jnp.concatenate([w[..., n:], w[..., :n]], axis=-1)


def _row_sum_mxu(sq, ones):
    return _dot(_bf(sq), ones)


def _mla_proj_kernel(x_ref, pos_ref, mixg_ref, win_ref, qlg_ref, kvlg_ref, wuq_ref, wukv_ref,
                     qg_ref, kg_ref, ones_ref, q_ref, k_ref, v_ref):
    xn = _bf(_rms(x_ref[...], mixg_ref[...]))
    lat = _dot(xn, win_ref[...])
    c_q = _bf(_rms(lat[:, :Q_LORA], qlg_ref[...]))
    c_kv = _bf(_rms(lat[:, Q_LORA:Q_LORA + KV_LORA], kvlg_ref[...]))
    k_rope = lat[:, Q_LORA + KV_LORA:]

    tm = x_ref.shape[0]
    half = D_ROPE // 2
    lane = lax.broadcasted_iota(jnp.int32, (tm, LANES), 1)
    inv_freq = jnp.exp((lane % half).astype(F32) * (-2.0 / D_ROPE * math.log(ROPE_THETA)))
    ang = pos_ref[...].astype(F32) * inv_freq
    keep = lane < D_ROPE
    cos = jnp.where(keep, jnp.cos(ang), 0.0)
    sin = jnp.sin(ang)
    sin = jnp.where(keep, jnp.where(lane < half, -sin, sin), 0.0)

    def rope(t):
        return t * cos + pltpu.roll(t, D_ROPE, 1) * sin

    sel = ones_ref[...]
    ones_rope = ones_ref[D_NOPE:2 * D_NOPE, :LANES]

    qg = qg_ref[...]
    kg = kg_ref[...]
    k_rot = rope(k_rope * kg[:, D_NOPE:])
    kr_ss = _row_sum_mxu(k_rope * k_rope, ones_rope)
    scale = QK_DIM ** -0.5
    for h in range(N_HEADS):
        cols = slice(h * QK_PAD, (h + 1) * QK_PAD)
        qh = _dot(c_q, wuq_ref[:, cols])
        kvh = _dot(c_kv, wukv_ref[:, cols])
        kn = kvh[:, :D_NOPE]
        ss = _row_sum_mxu(jnp.concatenate([qh * qh, kn * kn], axis=1), sel)
        rq = lax.rsqrt(ss[:, :LANES] * (1.0 / QK_DIM) + EPS) * scale
        rk = lax.rsqrt((ss[:, LANES:] + kr_ss) * (1.0 / QK_DIM) + EPS)
        qh = qh * qg
        q_ref[0, h, :, :D_NOPE] = _bf(qh[:, :D_NOPE] * rq)
        q_ref[0, h, :, D_NOPE:] = _bf(rope(qh[:, D_NOPE:]) * rq)
        k_ref[0, h, :, :D_NOPE] = _bf(kn * kg[:, :D_NOPE] * rk)
        k_ref[0, h, :, D_NOPE:] = _bf(k_rot * rk)
        v_ref[0, h, :, :] = _bf(kvh[:, D_NOPE:])


def _mla_proj(x, pos, mix_g, w_in, q_lat_g, kv_lat_g, w_uq, w_ukv, q_gain, k_gain, *, batch, tm=512):
    m, d = x.shape
    seq = m // batch
    n_s = seq // tm
    lat_w = Q_LORA + KV_LORA + LANES
    w_in_p = _bf(jnp.concatenate([w_in, _swap_halves_cols(w_in[:, Q_LORA + KV_LORA:])], axis=1))
    w_uq_h = w_uq.reshape(Q_LORA, N_HEADS, QK_DIM)
    w_uq_p = _bf(jnp.concatenate([w_uq_h, _swap_halves_cols(w_uq_h[..., D_NOPE:])], axis=-1)
                 .reshape(Q_LORA, N_HEADS * QK_PAD))
    qg_p = jnp.concatenate([q_gain, _swap_halves_cols(q_gain[D_NOPE:])]).reshape(1, QK_PAD)
    kg_p = jnp.concatenate([k_gain, _swap_halves_cols(k_gain[D_NOPE:])]).reshape(1, QK_PAD)
    sel_r = lax.broadcasted_iota(jnp.int32, (QK_PAD + D_NOPE, 2 * LANES), 0)
    sel_c = lax.broadcasted_iota(jnp.int32, (QK_PAD + D_NOPE, 2 * LANES), 1)
    ones = jnp.where(sel_c < LANES, sel_r < QK_DIM, sel_r >= QK_PAD).astype(BF16)
    hm = lambda w: pl.BlockSpec((1, N_HEADS, tm, w), lambda i: (i // n_s, 0, i % n_s, 0))
    return pl.pallas_call(
        _mla_proj_kernel,
        out_shape=(jax.ShapeDtypeStruct((batch, N_HEADS, seq, QK_PAD), BF16),
                   jax.ShapeDtypeStruct((batch, N_HEADS, seq, QK_PAD), BF16),
                   jax.ShapeDtypeStruct((batch, N_HEADS, seq, D_V), BF16)),
        grid=(m // tm,),
        in_specs=[
            pl.BlockSpec((tm, d), lambda i: (i, 0)),
            pl.BlockSpec((tm, 1), lambda i: (i, 0)),
            _const_spec((1, d)),
            _const_spec((d, lat_w)),
            _const_spec((1, Q_LORA)),
            _const_spec((1, KV_LORA)),
            _const_spec((Q_LORA, N_HEADS * QK_PAD)),
            _const_spec((KV_LORA, N_HEADS * (D_NOPE + D_V))),
            _const_spec((1, QK_PAD)),
            _const_spec((1, QK_PAD)),
            _const_spec((QK_PAD + D_NOPE, 2 * LANES)),
        ],
        out_specs=(hm(QK_PAD), hm(QK_PAD), hm(D_V)),
        compiler_params=_params(("parallel",)),
        name="mla_proj",
    )(x, pos.reshape(m, 1), mix_g.reshape(1, d), w_in_p, q_lat_g.reshape(1, Q_LORA),
      kv_lat_g.reshape(1, KV_LORA), w_uq_p, _bf(w_ukv), qg_p, kg_p, ones)


def _attn_kernel(q_ref, k_ref, v_ref, o_ref, *, seq, tq, hb):
    def scores(q, k):
        return lax.dot_general(q, k, (((1,), (1,)), ((), ())), preferred_element_type=F32)

    row = lax.broadcasted_iota(jnp.int32, (tq, tq), 0)
    col = lax.broadcasted_iota(jnp.int32, (tq, tq), 1)
    causal = col <= row
    ones = jnp.ones((seq, LANES), BF16)
    for hh in range(hb):
        v1 = jnp.concatenate([v_ref[0, hh], ones], axis=1)
        for qi in range(seq // tq):
            lo = qi * tq
            q = q_ref[0, hh, lo:lo + tq, :]
            s_d = jnp.where(causal, scores(q, k_ref[0, hh, lo:lo + tq, :]), -jnp.inf)
            m = jnp.max(s_d, axis=-1, keepdims=True)
            if qi:
                s_o = scores(q, k_ref[0, hh, :lo, :])
                m = jnp.maximum(m, jnp.max(s_o, axis=-1, keepdims=True))
            o = _dot(_bf(jnp.exp(s_d - m)), v1[lo:lo + tq])
            if qi:
                o = o + _dot(_bf(jnp.exp(s_o - m)), v1[:lo])
            o_ref[0, lo:lo + tq, hh * D_V:(hh + 1) * D_V] = _bf(o[:, :D_V] / o[:, D_V:])


def _attention(q, k, v, *, tq=512, hb=4):
    b, h, s, _ = q.shape
    return pl.pallas_call(
        functools.partial(_attn_kernel, seq=s, tq=tq, hb=hb),
        out_shape=jax.ShapeDtypeStruct((b, s, h * D_V), BF16),
        grid=(b, h // hb),
        in_specs=[
            pl.BlockSpec((1, hb, s, QK_PAD), lambda i, j: (i, j, 0, 0)),
            pl.BlockSpec((1, hb, s, QK_PAD), lambda i, j: (i, j, 0, 0)),
            pl.BlockSpec((1, hb, s, D_V), lambda i, j: (i, j, 0, 0)),
        ],
        out_specs=pl.BlockSpec((1, s, hb * D_V), lambda i, j: (i, 0, j)),
        compiler_params=_params(("parallel", "parallel")),
        name="attention",
    )(q, k, v)


def _proj_res_kernel(a_ref, w_ref, h_ref, o_ref):
    o_ref[...] = h_ref[...] + _dot(a_ref[...], _bf(w_ref[...]))


def _proj_residual(a, w, h, *, tm=512):
    m, d = h.shape
    k = a.shape[1]
    return pl.pallas_call(
        _proj_res_kernel,
        out_shape=jax.ShapeDtypeStruct((m, d), F32),
        grid=(m // tm,),
        in_specs=[
            pl.BlockSpec((tm, k), lambda i: (i, 0)),
            _const_spec((k, d)),
            pl.BlockSpec((tm, d), lambda i: (i, 0)),
        ],
        out_specs=pl.BlockSpec((tm, d), lambda i: (i, 0)),
        compiler_params=_params(("parallel",)),
        name="attn_out_proj",
    )(a, w, h)


def _glu_kernel(x_ref, g_ref, wa_ref, wg_ref, ba_ref, bg_ref, u_ref, xn_ref):
    @pl.when(pl.program_id(1) == 0)
    def _():
        xn_ref[...] = _bf(_rms(x_ref[...], g_ref[...]))

    xn = xn_ref[...]
    a = _dot(xn, _bf(wa_ref[...])) + ba_ref[...]
    g = _dot(xn, _bf(wg_ref[...])) + bg_ref[...]
    u_ref[...] = _bf(a * jax.nn.sigmoid(g))


def _glu(x, norm_g, w, b, *, tm=1024, tn=512):
    m, d = x.shape
    c = w.shape[1] // 2
    n_c = c // tn
    b2 = b.reshape(1, 2 * c)
    return pl.pallas_call(
        _glu_kernel,
        out_shape=jax.ShapeDtypeStruct((m, c), BF16),
        grid=(m // tm, n_c),
        in_specs=[
            pl.BlockSpec((tm, d), lambda i, j: (i, 0)),
            pl.BlockSpec((1, d), lambda i, j: (0, 0)),
            pl.BlockSpec((d, tn), lambda i, j: (0, j)),
            pl.BlockSpec((d, tn), lambda i, j: (0, j + n_c)),
            pl.BlockSpec((1, tn), lambda i, j: (0, j)),
            pl.BlockSpec((1, tn), lambda i, j: (0, j + n_c)),
        ],
        out_specs=pl.BlockSpec((tm, tn), lambda i, j: (i, j)),
        scratch_shapes=[pltpu.VMEM((tm, d), BF16)],
        compiler_params=_params(("parallel", "arbitrary")),
        name="conv_glu",
    )(x, norm_g.reshape(1, d), w, w, b2, b2)


def _dft_tables(ts):
    n = CONV_HIST + ts
    n_bins = n // 2 + 1
    n_bins_p = -(-n_bins // LANES) * LANES
    f = np.arange(n_bins_p, dtype=np.float64)[:, None]
    valid = (f < n_bins).astype(np.float64)
    r = np.arange(n, dtype=np.float64)[None, :]
    ang = 2.0 * np.pi * f * r / n
    fwd = np.concatenate([np.cos(ang) * valid, -np.sin(ang) * valid], axis=0)
    weight = np.where((f == 0) | (f == n // 2), 1.0, 2.0) * valid / n
    ang_o = ang[:, CONV_HIST:]
    inv = np.concatenate([(np.cos(ang_o) * weight).T, (-np.sin(ang_o) * weight).T], axis=1)
    delay = (CONV_WIDTH - 1) - np.arange(CONV_TAPS_P, dtype=np.float64)[None, :]
    tap_ok = (np.arange(CONV_TAPS_P) < CONV_WIDTH).astype(np.float64)[None, :]
    ang_t = 2.0 * np.pi * f * delay / n
    taps = np.concatenate([np.cos(ang_t) * valid * tap_ok, -np.sin(ang_t) * valid * tap_ok], axis=0)
    return fwd, inv, taps


def _conv_kernel(u_ref, hist_ref, fwd_ref, inv_ref, taps_ref, wdw_ref, bdw_ref, lng_ref, lnb_ref,
                 w2_ref, h_ref, o_ref, spec_ref, y_ref, *, cn):
    @pl.when((pl.program_id(0) == 0) & (pl.program_id(1) == 0))
    def _():
        spec_ref[...] = jnp.dot(taps_ref[...], wdw_ref[...], precision=lax.Precision.HIGHEST,
                                preferred_element_type=F32)

    n_bins_p = fwd_ref.shape[0] // 2
    have_hist = pl.program_id(1) > 0
    c = u_ref.shape[2]
    for c0 in range(0, c, cn):
        cols = slice(c0, c0 + cn)
        hist = hist_ref[0, :, cols]
        x = jnp.concatenate([jnp.where(have_hist, hist, jnp.zeros_like(hist)),
                             u_ref[0, :, cols]], axis=0)
        xs = _dot(fwd_ref[...], x)
        xr, xi = xs[:n_bins_p], xs[n_bins_p:]
        fr, fi = spec_ref[:n_bins_p, cols], spec_ref[n_bins_p:, cols]
        ys = jnp.concatenate([_bf(xr * fr - xi * fi), _bf(xr * fi + xi * fr)], axis=0)
        y_ref[:, cols] = _dot(inv_ref[...], ys) + bdw_ref[:, cols]

    y = y_ref[...]
    mu = jnp.mean(y, axis=-1, keepdims=True)
    yc = y - mu
    yn = yc * lax.rsqrt(jnp.mean(yc * yc, axis=-1, keepdims=True) + EPS) * lng_ref[...] + lnb_ref[...]
    act = _bf(yn * jax.nn.sigmoid(yn))
    o_ref[0] = h_ref[0] + _dot(act, _bf(w2_ref[...]))


def _conv_module(u, w_dw, b_dw, ln_g, ln_b, w2, h, *, ts=256, cn=512):
    b, s, c = u.shape
    d = w2.shape[1]
    r = ts // CONV_HIST
    fwd, inv, taps = _dft_tables(ts)
    n_spec = fwd.shape[0]
    row = lambda i, j: (i, j, 0)
    return pl.pallas_call(
        functools.partial(_conv_kernel, cn=cn),
        out_shape=jax.ShapeDtypeStruct((b, s, d), F32),
        grid=(b, s // ts),
        in_specs=[
            pl.BlockSpec((1, ts, c), row),
            pl.BlockSpec((1, CONV_HIST, c), lambda i, j: (i, jnp.maximum(j * r - 1, 0), 0)),
            _const_spec(fwd.shape),
            _const_spec(inv.shape),
            _const_spec(taps.shape),
            _const_spec((CONV_TAPS_P, c)),
            _const_spec((1, c)),
            _const_spec((1, c)),
            _const_spec((1, c)),
            _const_spec((c, d)),
            pl.BlockSpec((1, ts, d), row),
        ],
        out_specs=pl.BlockSpec((1, ts, d), row),
        scratch_shapes=[pltpu.VMEM((n_spec, c), F32), pltpu.VMEM((ts, c), F32)],
        compiler_params=_params(("arbitrary", "arbitrary")),
        name="conv_module",
    )(u, u, jnp.asarray(fwd, BF16), jnp.asarray(inv, BF16), jnp.asarray(taps, F32),
      jnp.pad(w_dw, ((0, CONV_TAPS_P - CONV_WIDTH), (0, 0))), b_dw.reshape(1, c),
      ln_g.reshape(1, c), ln_b.reshape(1, c), w2, h)


def _ple_kernel(h_ref, p_ref, wp_ref, eg_ref, gg_ref, wg_ref, o_ref):
    h = h_ref[...]
    e = _rms(_dot(_bf(p_ref[...]), _bf(wp_ref[...])), eg_ref[...])
    gate = jax.nn.sigmoid(_dot(_bf(_rms(h, gg_ref[...])), _bf(wg_ref[...])))
    o_ref[...] = h + e * gate


def _ple(h, p, w_proj, e_norm, gate_norm, w_gate, layer, *, tm=512):
    m, d = h.shape
    return pl.pallas_call(
        _ple_kernel,
        out_shape=jax.ShapeDtypeStruct((m, d), F32),
        grid=(m // tm,),
        in_specs=[
            pl.BlockSpec((tm, d), lambda i: (i, 0)),
            pl.BlockSpec((None, tm, D_PLE), lambda i: (layer, i, 0)),
            _const_spec((D_PLE, d), layer),
            _const_spec((1, d), layer),
            _const_spec((1, d), layer),
            _const_spec((d, d), layer),
        ],
        out_specs=pl.BlockSpec((tm, d), lambda i: (i, 0)),
        compiler_params=_params(("parallel",)),
        name="ple",
    )(h, p, w_proj, e_norm.reshape(-1, 1, d), gate_norm.reshape(-1, 1, d), w_gate)


def kernel(x, p, positions, ffn_a_norm, ffn_a_w_in, ffn_a_w_out, ffn_b_norm, ffn_b_w_in, ffn_b_w_out,
           mix_norm, mla_w_in, mla_q_lat_norm, mla_kv_lat_norm, mla_w_uq, mla_w_ukv, mla_q_gain,
           mla_k_gain, mla_w_o, conv_w_pw1, conv_b_pw1, conv_w_dw, conv_b_dw, conv_ln_g, conv_ln_b,
           conv_w_pw2, ple_w_proj, ple_norm, ple_gate_norm, ple_w_gate):
    batch, seq, d = x.shape
    m = batch * seq
    depth = ffn_a_norm.shape[0]
    h = x.reshape(m, d)
    pos = positions.reshape(m)
    p2 = p.reshape(depth, m, -1)
    for i in range(depth):
        h = _ffn(h, ffn_a_norm, ffn_a_w_in, ffn_a_w_out, i)
        j = i // 2
        if i % 2 == 0:
            q, k, v = _mla_proj(h, pos, mix_norm[i], mla_w_in[j], mla_q_lat_norm[j], mla_kv_lat_norm[j],
                                mla_w_uq[j], mla_w_ukv[j], mla_q_gain[j], mla_k_gain[j], batch=batch)
            o = _attention(q, k, v)
            h = _proj_residual(o.reshape(m, N_HEADS * D_V), mla_w_o[j], h)
        else:
            u = _glu(h, mix_norm[i], conv_w_pw1[j], conv_b_pw1[j])
            h = _conv_module(u.reshape(batch, seq, -1), conv_w_dw[j], conv_b_dw[j], conv_ln_g[j],
                             conv_ln_b[j], conv_w_pw2[j], h.reshape(batch, seq, d)).reshape(m, d)
        h = _ffn(h, ffn_b_norm, ffn_b_w_in, ffn_b_w_out, i)
        h = _ple(h, p2, ple_w_proj, ple_norm, ple_gate_norm, ple_w_gate, i)
    return h.reshape(batch, seq, d)
```

```python
import functools
import math

import jax
import jax.numpy as jnp
import numpy as np
from jax import lax
from jax.experimental import pallas as pl
from jax.experimental.pallas import tpu as pltpu

D_MODEL = 2048
N_HEADS = 16
Q_LORA = 512
KV_LORA = 512
D_NOPE = 128
D_ROPE = 64
D_V = 128
QK_DIM = D_NOPE + D_ROPE
ROPE_THETA = 10000.0
CONV_WIDTH = 31
D_FF = 5632
D_PLE = 256
EPS = 1e-6
FFN_RESIDUAL_WEIGHT = 0.5

LANES = 128
V7X_VMEM_LIMIT_BYTES = 60 * 1024 * 1024

QK_PAD = 2 * LANES
CONV_HIST = LANES
CONV_TAPS_P = 32

F32 = jnp.float32
BF16 = jnp.bfloat16


def _bf(x):
    return x.astype(BF16)


def _dot(a, b):
    return jnp.dot(a, b, preferred_element_type=F32)


def _rms(x, g):
    ms = jnp.mean(x * x, axis=-1, keepdims=True)
    return x * lax.rsqrt(ms + EPS) * g


def _params(sem):
    return pltpu.CompilerParams(dimension_semantics=sem,
                                vmem_limit_bytes=V7X_VMEM_LIMIT_BYTES)


def _const_spec(shape, layer=None):
    nd = len(shape)
    if layer is None:
        return pl.BlockSpec(shape, lambda *_: (0,) * nd, pipeline_mode=pl.Buffered(1))
    return pl.BlockSpec((None,) + shape, lambda *_: (layer,) + (0,) * nd,
                        pipeline_mode=pl.Buffered(1))


def _ffn_kernel(x_ref, g_ref, wg_ref, wu_ref, wo_ref, o_ref, xn_ref):
    @pl.when(pl.program_id(1) == 0)
    def _():
        x = x_ref[...]
        xn_ref[...] = _bf(_rms(x, g_ref[...]))
        o_ref[...] = x

    xn = xn_ref[...]
    g = _dot(xn, _bf(wg_ref[...]))
    u = _dot(xn, _bf(wu_ref[...]))
    a = _bf(g * jax.nn.sigmoid(g) * u)
    o_ref[...] += FFN_RESIDUAL_WEIGHT * _dot(a, _bf(wo_ref[...]))


def _ffn(x, norm_g, w_in, w_out, layer, *, tm=1024, tf=256):
    m, d = x.shape
    n_f = D_FF // tf
    return pl.pallas_call(
        _ffn_kernel,
        out_shape=jax.ShapeDtypeStruct((m, d), F32),
        grid=(m // tm, n_f),
        in_specs=[
            pl.BlockSpec((tm, d), lambda i, j: (i, 0)),
            pl.BlockSpec((None, 1, d), lambda i, j: (layer, 0, 0)),
            pl.BlockSpec((None, d, tf), lambda i, j: (layer, 0, j)),
            pl.BlockSpec((None, d, tf), lambda i, j: (layer, 0, j + n_f)),
            pl.BlockSpec((None, tf, d), lambda i, j: (layer, j, 0)),
        ],
        out_specs=pl.BlockSpec((tm, d), lambda i, j: (i, 0)),
        scratch_shapes=[pltpu.VMEM((tm, d), BF16)],
        compiler_params=_params(("parallel", "arbitrary")),
        name="ffn",
    )(x, norm_g.reshape(-1, 1, d), w_in, w_in, w_out)


def _swap_halves_cols(w):
    n = w.shape[-1] // 2
    return jnp.concatenate([w[..., n:], w[..., :n]], axis=-1)


def _row_sum_mxu(sq, ones):
    return _dot(_bf(sq), ones)


def _mla_proj_kernel(x_ref, pos_ref, mixg_ref, win_ref, qlg_ref, kvlg_ref, wuq_ref, wukv_ref,
                     qg_ref, kg_ref, ones_ref, q_ref, k_ref, v_ref):
    xn = _bf(_rms(x_ref[...], mixg_ref[...]))
    lat = _dot(xn, win_ref[...])
    c_q = _bf(_rms(lat[:, :Q_LORA], qlg_ref[...]))
    c_kv = _bf(_rms(lat[:, Q_LORA:Q_LORA + KV_LORA], kvlg_ref[...]))
    k_rope = lat[:, Q_LORA + KV_LORA:]

    tm = x_ref.shape[0]
    half = D_ROPE // 2
    lane = lax.broadcasted_iota(jnp.int32, (tm, LANES), 1)
    inv_freq = jnp.exp((lane % half).astype(F32) * (-2.0 / D_ROPE * math.log(ROPE_THETA)))
    ang = pos_ref[...].astype(F32) * inv_freq
    keep = lane < D_ROPE
    cos = jnp.where(keep, jnp.cos(ang), 0.0)
    sin = jnp.sin(ang)
    sin = jnp.where(keep, jnp.where(lane < half, -sin, sin), 0.0)

    def rope(t):
        return t * cos + pltpu.roll(t, D_ROPE, 1) * sin

    sel = ones_ref[...]
    ones_rope = ones_ref[D_NOPE:2 * D_NOPE, :LANES]

    qg = qg_ref[...]
    kg = kg_ref[...]
    k_rot = rope(k_rope * kg[:, D_NOPE:])
    kr_ss = _row_sum_mxu(k_rope * k_rope, ones_rope)
    scale = QK_DIM ** -0.5
    for h in range(N_HEADS):
        cols = slice(h * QK_PAD, (h + 1) * QK_PAD)
        qh = _dot(c_q, wuq_ref[:, cols])
        kvh = _dot(c_kv, wukv_ref[:, cols])
        kn = kvh[:, :D_NOPE]
        ss = _row_sum_mxu(jnp.concatenate([qh * qh, kn * kn], axis=1), sel)
        rq = lax.rsqrt(ss[:, :LANES] * (1.0 / QK_DIM) + EPS) * scale
        rk = lax.rsqrt((ss[:, LANES:] + kr_ss) * (1.0 / QK_DIM) + EPS)
        qh = qh * qg
        q_ref[0, h, :, :D_NOPE] = _bf(qh[:, :D_NOPE] * rq)
        q_ref[0, h, :, D_NOPE:] = _bf(rope(qh[:, D_NOPE:]) * rq)
        k_ref[0, h, :, :D_NOPE] = _bf(kn * kg[:, :D_NOPE] * rk)
        k_ref[0, h, :, D_NOPE:] = _bf(k_rot * rk)
        v_ref[0, h, :, :] = _bf(kvh[:, D_NOPE:])


def _mla_proj(x, pos, mix_g, w_in, q_lat_g, kv_lat_g, w_uq, w_ukv, q_gain, k_gain, *, batch, tm=512):
    m, d = x.shape
    seq = m // batch
    n_s = seq // tm
    lat_w = Q_LORA + KV_LORA + LANES
    w_in_p = _bf(jnp.concatenate([w_in, _swap_halves_cols(w_in[:, Q_LORA + KV_LORA:])], axis=1))
    w_uq_h = w_uq.reshape(Q_LORA, N_HEADS, QK_DIM)
    w_uq_p = _bf(jnp.concatenate([w_uq_h, _swap_halves_cols(w_uq_h[..., D_NOPE:])], axis=-1)
                 .reshape(Q_LORA, N_HEADS * QK_PAD))
    qg_p = jnp.concatenate([q_gain, _swap_halves_cols(q_gain[D_NOPE:])]).reshape(1, QK_PAD)
    kg_p = jnp.concatenate([k_gain, _swap_halves_cols(k_gain[D_NOPE:])]).reshape(1, QK_PAD)
    sel_r = lax.broadcasted_iota(jnp.int32, (QK_PAD + D_NOPE, 2 * LANES), 0)
    sel_c = lax.broadcasted_iota(jnp.int32, (QK_PAD + D_NOPE, 2 * LANES), 1)
    ones = jnp.where(sel_c < LANES, sel_r < QK_DIM, sel_r >= QK_PAD).astype(BF16)
    hm = lambda w: pl.BlockSpec((1, N_HEADS, tm, w), lambda i: (i // n_s, 0, i % n_s, 0))
    return pl.pallas_call(
        _mla_proj_kernel,
        out_shape=(jax.ShapeDtypeStruct((batch, N_HEADS, seq, QK_PAD), BF16),
                   jax.ShapeDtypeStruct((batch, N_HEADS, seq, QK_PAD), BF16),
                   jax.ShapeDtypeStruct((batch, N_HEADS, seq, D_V), BF16)),
        grid=(m // tm,),
        in_specs=[
            pl.BlockSpec((tm, d), lambda i: (i, 0)),
            pl.BlockSpec((tm, 1), lambda i: (i, 0)),
            _const_spec((1, d)),
            _const_spec((d, lat_w)),
            _const_spec((1, Q_LORA)),
            _const_spec((1, KV_LORA)),
            _const_spec((Q_LORA, N_HEADS * QK_PAD)),
            _const_spec((KV_LORA, N_HEADS * (D_NOPE + D_V))),
            _const_spec((1, QK_PAD)),
            _const_spec((1, QK_PAD)),
            _const_spec((QK_PAD + D_NOPE, 2 * LANES)),
        ],
        out_specs=(hm(QK_PAD), hm(QK_PAD), hm(D_V)),
        compiler_params=_params(("parallel",)),
        name="mla_proj",
    )(x, pos.reshape(m, 1), mix_g.reshape(1, d), w_in_p, q_lat_g.reshape(1, Q_LORA),
      kv_lat_g.reshape(1, KV_LORA), w_uq_p, _bf(w_ukv), qg_p, kg_p, ones)


def _attn_kernel(q_ref, k_ref, v_ref, o_ref, *, seq, tq, hb):
    def scores(q, k):
        return lax.dot_general(q, k, (((1,), (1,)), ((), ())), preferred_element_type=F32)

    row = lax.broadcasted_iota(jnp.int32, (tq, tq), 0)
    col = lax.broadcasted_iota(jnp.int32, (tq, tq), 1)
    causal = col <= row
    ones = jnp.ones((seq, LANES), BF16)
    for hh in range(hb):
        v1 = jnp.concatenate([v_ref[0, hh], ones], axis=1)
        for qi in range(seq // tq):
            lo = qi * tq
            q = q_ref[0, hh, lo:lo + tq, :]
            s_d = jnp.where(causal, scores(q, k_ref[0, hh, lo:lo + tq, :]), -jnp.inf)
            m = jnp.max(s_d, axis=-1, keepdims=True)
            if qi:
                s_o = scores(q, k_ref[0, hh, :lo, :])
                m = jnp.maximum(m, jnp.max(s_o, axis=-1, keepdims=True))
            o = _dot(_bf(jnp.exp(s_d - m)), v1[lo:lo + tq])
            if qi:
                o = o + _dot(_bf(jnp.exp(s_o - m)), v1[:lo])
            o_ref[0, lo:lo + tq, hh * D_V:(hh + 1) * D_V] = _bf(o[:, :D_V] / o[:, D_V:])


def _attention(q, k, v, *, tq=512, hb=4):
    b, h, s, _ = q.shape
    return pl.pallas_call(
        functools.partial(_attn_kernel, seq=s, tq=tq, hb=hb),
        out_shape=jax.ShapeDtypeStruct((b, s, h * D_V), BF16),
        grid=(b, h // hb),
        in_specs=[
            pl.BlockSpec((1, hb, s, QK_PAD), lambda i, j: (i, j, 0, 0)),
            pl.BlockSpec((1, hb, s, QK_PAD), lambda i, j: (i, j, 0, 0)),
            pl.BlockSpec((1, hb, s, D_V), lambda i, j: (i, j, 0, 0)),
        ],
        out_specs=pl.BlockSpec((1, s, hb * D_V), lambda i, j: (i, 0, j)),
        compiler_params=_params(("parallel", "parallel")),
        name="attention",
    )(q, k, v)


def _proj_res_kernel(a_ref, w_ref, h_ref, o_ref):
    o_ref[...] = h_ref[...] + _dot(a_ref[...], _bf(w_ref[...]))


def _proj_residual(a, w, h, *, tm=512):
    m, d = h.shape
    k = a.shape[1]
    return pl.pallas_call(
        _proj_res_kernel,
        out_shape=jax.ShapeDtypeStruct((m, d), F32),
        grid=(m // tm,),
        in_specs=[
            pl.BlockSpec((tm, k), lambda i: (i, 0)),
            _const_spec((k, d)),
            pl.BlockSpec((tm, d), lambda i: (i, 0)),
        ],
        out_specs=pl.BlockSpec((tm, d), lambda i: (i, 0)),
        compiler_params=_params(("parallel",)),
        name="attn_out_proj",
    )(a, w, h)


def _glu_kernel(x_ref, g_ref, wa_ref, wg_ref, ba_ref, bg_ref, u_ref, xn_ref):
    @pl.when(pl.program_id(1) == 0)
    def _():
        xn_ref[...] = _bf(_rms(x_ref[...], g_ref[...]))

    xn = xn_ref[...]
    a = _dot(xn, _bf(wa_ref[...])) + ba_ref[...]
    g = _dot(xn, _bf(wg_ref[...])) + bg_ref[...]
    u_ref[...] = _bf(a * jax.nn.sigmoid(g))


def _glu(x, norm_g, w, b, *, tm=1024, tn=512):
    m, d = x.shape
    c = w.shape[1] // 2
    n_c = c // tn
    b2 = b.reshape(1, 2 * c)
    return pl.pallas_call(
        _glu_kernel,
        out_shape=jax.ShapeDtypeStruct((m, c), BF16),
        grid=(m // tm, n_c),
        in_specs=[
            pl.BlockSpec((tm, d), lambda i, j: (i, 0)),
            pl.BlockSpec((1, d), lambda i, j: (0, 0)),
            pl.BlockSpec((d, tn), lambda i, j: (0, j)),
            pl.BlockSpec((d, tn), lambda i, j: (0, j + n_c)),
            pl.BlockSpec((1, tn), lambda i, j: (0, j)),
            pl.BlockSpec((1, tn), lambda i, j: (0, j + n_c)),
        ],
        out_specs=pl.BlockSpec((tm, tn), lambda i, j: (i, j)),
        scratch_shapes=[pltpu.VMEM((tm, d), BF16)],
        compiler_params=_params(("parallel", "arbitrary")),
        name="conv_glu",
    )(x, norm_g.reshape(1, d), w, w, b2, b2)


def _dft_tables(ts):
    n = CONV_HIST + ts
    n_bins = n // 2 + 1
    n_bins_p = -(-n_bins // LANES) * LANES
    f = np.arange(n_bins_p, dtype=np.float64)[:, None]
    valid = (f < n_bins).astype(np.float64)
    r = np.arange(n, dtype=np.float64)[None, :]
    ang = 2.0 * np.pi * f * r / n
    fwd = np.concatenate([np.cos(ang) * valid, -np.sin(ang) * valid], axis=0)
    weight = np.where((f == 0) | (f == n // 2), 1.0, 2.0) * valid / n
    ang_o = ang[:, CONV_HIST:]
    inv = np.concatenate([(np.cos(ang_o) * weight).T, (-np.sin(ang_o) * weight).T], axis=1)
    delay = (CONV_WIDTH - 1) - np.arange(CONV_TAPS_P, dtype=np.float64)[None, :]
    tap_ok = (np.arange(CONV_TAPS_P) < CONV_WIDTH).astype(np.float64)[None, :]
    ang_t = 2.0 * np.pi * f * delay / n
    taps = np.concatenate([np.cos(ang_t) * valid * tap_ok, -np.sin(ang_t) * valid * tap_ok], axis=0)
    return fwd, inv, taps


def _conv_kernel(u_ref, hist_ref, fwd_ref, inv_ref, taps_ref, wdw_ref, bdw_ref, lng_ref, lnb_ref,
                 w2_ref, h_ref, o_ref, spec_ref, y_ref, *, cn):
    @pl.when((pl.program_id(0) == 0) & (pl.program_id(1) == 0))
    def _():
        spec_ref[...] = jnp.dot(taps_ref[...], wdw_ref[...], precision=lax.Precision.HIGHEST,
                                preferred_element_type=F32)

    n_bins_p = fwd_ref.shape[0] // 2
    have_hist = pl.program_id(1) > 0
    c = u_ref.shape[2]
    for c0 in range(0, c, cn):
        cols = slice(c0, c0 + cn)
        hist = hist_ref[0, :, cols]
        x = jnp.concatenate([jnp.where(have_hist, hist, jnp.zeros_like(hist)),
                             u_ref[0, :, cols]], axis=0)
        xs = _dot(fwd_ref[...], x)
        xr, xi = xs[:n_bins_p], xs[n_bins_p:]
        fr, fi = spec_ref[:n_bins_p, cols], spec_ref[n_bins_p:, cols]
        ys = jnp.concatenate([_bf(xr * fr - xi * fi), _bf(xr * fi + xi * fr)], axis=0)
        y_ref[:, cols] = _dot(inv_ref[...], ys) + bdw_ref[:, cols]

    y = y_ref[...]
    mu = jnp.mean(y, axis=-1, keepdims=True)
    yc = y - mu
    yn = yc * lax.rsqrt(jnp.mean(yc * yc, axis=-1, keepdims=True) + EPS) * lng_ref[...] + lnb_ref[...]
    act = _bf(yn * jax.nn.sigmoid(yn))
    o_ref[0] = h_ref[0] + _dot(act, _bf(w2_ref[...]))


def _conv_module(u, w_dw, b_dw, ln_g, ln_b, w2, h, *, ts=256, cn=1024):
    b, s, c = u.shape
    d = w2.shape[1]
    r = ts // CONV_HIST
    fwd, inv, taps = _dft_tables(ts)
    n_spec = fwd.shape[0]
    row = lambda i, j: (i, j, 0)
    return pl.pallas_call(
        functools.partial(_conv_kernel, cn=cn),
        out_shape=jax.ShapeDtypeStruct((b, s, d), F32),
        grid=(b, s // ts),
        in_specs=[
            pl.BlockSpec((1, ts, c), row),
            pl.BlockSpec((1, CONV_HIST, c), lambda i, j: (i, jnp.maximum(j * r - 1, 0), 0)),
            _const_spec(fwd.shape),
            _const_spec(inv.shape),
            _const_spec(taps.shape),
            _const_spec((CONV_TAPS_P, c)),
            _const_spec((1, c)),
            _const_spec((1, c)),
            _const_spec((1, c)),
            _const_spec((c, d)),
            pl.BlockSpec((1, ts, d), row),
        ],
        out_specs=pl.BlockSpec((1, ts, d), row),
        scratch_shapes=[pltpu.VMEM((n_spec, c), F32), pltpu.VMEM((ts, c), F32)],
        compiler_params=_params(("arbitrary", "arbitrary")),
        name="conv_module",
    )(u, u, jnp.asarray(fwd, BF16), jnp.asarray(inv, BF16), jnp.asarray(taps, F32),
      jnp.pad(w_dw, ((0, CONV_TAPS_P - CONV_WIDTH), (0, 0))), b_dw.reshape(1, c),
      ln_g.reshape(1, c), ln_b.reshape(1, c), w2, h)


def _ple_kernel(h_ref, p_ref, wp_ref, eg_ref, gg_ref, wg_ref, o_ref):
    h = h_ref[...]
    e = _rms(_dot(_bf(p_ref[...]), _bf(wp_ref[...])), eg_ref[...])
    gate = jax.nn.sigmoid(_dot(_bf(_rms(h, gg_ref[...])), _bf(wg_ref[...])))
    o_ref[...] = h + e * gate


def _ple(h, p, w_proj, e_norm, gate_norm, w_gate, layer, *, tm=512):
    m, d = h.shape
    return pl.pallas_call(
        _ple_kernel,
        out_shape=jax.ShapeDtypeStruct((m, d), F32),
        grid=(m // tm,),
        in_specs=[
            pl.BlockSpec((tm, d), lambda i: (i, 0)),
            pl.BlockSpec((None, tm, D_PLE), lambda i: (layer, i, 0)),
            _const_spec((D_PLE, d), layer),
            _const_spec((1, d), layer),
            _const_spec((1, d), layer),
            _const_spec((d, d), layer),
        ],
        out_specs=pl.BlockSpec((tm, d), lambda i: (i, 0)),
        compiler_params=_params(("parallel",)),
        name="ple",
    )(h, p, w_proj, e_norm.reshape(-1, 1, d), gate_norm.reshape(-1, 1, d), w_gate)


def kernel(x, p, positions, ffn_a_norm, ffn_a_w_in, ffn_a_w_out, ffn_b_norm, ffn_b_w_in, ffn_b_w_out,
           mix_norm, mla_w_in, mla_q_lat_norm, mla_kv_lat_norm, mla_w_uq, mla_w_ukv, mla_q_gain,
           mla_k_gain, mla_w_o, conv_w_pw1, conv_b_pw1, conv_w_dw, conv_b_dw, conv_ln_g, conv_ln_b,
           conv_w_pw2, ple_w_proj, ple_norm, ple_gate_norm, ple_w_gate):
    batch, seq, d = x.shape
    m = batch * seq
    depth = ffn_a_norm.shape[0]
    h = x.reshape(m, d)
    pos = positions.reshape(m)
    p2 = p.reshape(depth, m, -1)
    for i in range(depth):
        h = _ffn(h, ffn_a_norm, ffn_a_w_in, ffn_a_w_out, i)
        j = i // 2
        if i % 2 == 0:
            q, k, v = _mla_proj(h, pos, mix_norm[i], mla_w_in[j], mla_q_lat_norm[j], mla_kv_lat_norm[j],
                                mla_w_uq[j], mla_w_ukv[j], mla_q_gain[j], mla_k_gain[j], batch=batch)
            o = _attention(q, k, v)
            h = _proj_residual(o.reshape(m, N_HEADS * D_V), mla_w_o[j], h)
        else:
            u = _glu(h, mix_norm[i], conv_w_pw1[j], conv_b_pw1[j])
            h = _conv_module(u.reshape(batch, seq, -1), conv_w_dw[j], conv_b_dw[j], conv_ln_g[j],
                             conv_ln_b[j], conv_w_pw2[j], h.reshape(batch, seq, d)).reshape(m, d)
        h = _ffn(h, ffn_b_norm, ffn_b_w_in, ffn_b_w_out, i)
        h = _ple(h, p2, ple_w_proj, ple_norm, ple_gate_norm, ple_w_gate, i)
    return h.reshape(batch, seq, d)
```

```python
import functools
import math

import jax
import jax.numpy as jnp
import numpy as np
from jax import lax
from jax.experimental import pallas as pl
from jax.experimental.pallas import tpu as pltpu

D_MODEL = 2048
N_HEADS = 16
Q_LORA = 512
KV_LORA = 512
D_NOPE = 128
D_ROPE = 64
D_V = 128
QK_DIM = D_NOPE + D_ROPE
ROPE_THETA = 10000.0
CONV_WIDTH = 31
D_FF = 5632
D_PLE = 256
EPS = 1e-6
FFN_RESIDUAL_WEIGHT = 0.5

LANES = 128
V7X_VMEM_LIMIT_BYTES = 60 * 1024 * 1024

QK_PAD = 2 * LANES
CONV_HIST = LANES
CONV_TAPS_P = 32

F32 = jnp.float32
BF16 = jnp.bfloat16


def _bf(x):
    return x.astype(BF16)


def _dot(a, b):
    return jnp.dot(a, b, preferred_element_type=F32)


def _rms(x, g):
    ms = jnp.mean(x * x, axis=-1, keepdims=True)
    return x * lax.rsqrt(ms + EPS) * g


def _params(sem):
    return pltpu.CompilerParams(dimension_semantics=sem,
                                vmem_limit_bytes=V7X_VMEM_LIMIT_BYTES)


def _const_spec(shape, layer=None):
    nd = len(shape)
    if layer is None:
        return pl.BlockSpec(shape, lambda *_: (0,) * nd, pipeline_mode=pl.Buffered(1))
    return pl.BlockSpec((None,) + shape, lambda *_: (layer,) + (0,) * nd,
                        pipeline_mode=pl.Buffered(1))


def _ffn_first_kernel(x_ref, g_ref, wg_ref, wu_ref, wo_ref, o_ref, wgb_ref, wub_ref, wob_ref, xn_ref):
    @pl.when(pl.program_id(0) == 0)
    def _():
        x = x_ref[...]
        xn_ref[...] = _bf(_rms(x, g_ref[...]))
        o_ref[...] = x

    wg, wu, wo = _bf(wg_ref[...]), _bf(wu_ref[...]), _bf(wo_ref[...])
    wgb_ref[...] = wg
    wub_ref[...] = wu
    wob_ref[...] = wo
    xn = xn_ref[...]
    g = _dot(xn, wg)
    u = _dot(xn, wu)
    a = _bf(g * jax.nn.sigmoid(g) * u)
    o_ref[...] += FFN_RESIDUAL_WEIGHT * _dot(a, wo)


def _ffn_rest_kernel(x_ref, h0_ref, g_ref, wg_ref, wu_ref, wo_ref, o_ref, xn_ref):
    i = pl.program_id(0)
    j = pl.program_id(1)
    rows = h0_ref.shape[0]

    @pl.when((i == 0) & (j < o_ref.shape[0] // rows))
    def _():
        o_ref[pl.ds(pl.multiple_of(j * rows, rows), rows), :] = h0_ref[...]

    @pl.when((i > 0) & (j == 0))
    def _():
        x = x_ref[...]
        xn_ref[...] = _bf(_rms(x, g_ref[...]))
        o_ref[...] = x

    @pl.when(i > 0)
    def _():
        xn = xn_ref[...]
        g = _dot(xn, wg_ref[...])
        u = _dot(xn, wu_ref[...])
        a = _bf(g * jax.nn.sigmoid(g) * u)
        o_ref[...] += FFN_RESIDUAL_WEIGHT * _dot(a, wo_ref[...])


def _ffn(x, norm_g, w_in, w_out, layer, *, tm=1024, tf=256, tf2=512, rows=128):
    m, d = x.shape
    n_f = D_FF // tf
    n_f2 = D_FF // tf2
    n_copy = tm // rows
    g3 = norm_g.reshape(-1, 1, d)
    h0, wgb, wub, wob = pl.pallas_call(
        _ffn_first_kernel,
        out_shape=(jax.ShapeDtypeStruct((tm, d), F32), jax.ShapeDtypeStruct((d, D_FF), BF16),
                   jax.ShapeDtypeStruct((d, D_FF), BF16), jax.ShapeDtypeStruct((D_FF, d), BF16)),
        grid=(n_f,),
        in_specs=[
            pl.BlockSpec((tm, d), lambda j: (0, 0)),
            pl.BlockSpec((None, 1, d), lambda j: (layer, 0, 0)),
            pl.BlockSpec((None, d, tf), lambda j: (layer, 0, j)),
            pl.BlockSpec((None, d, tf), lambda j: (layer, 0, j + n_f)),
            pl.BlockSpec((None, tf, d), lambda j: (layer, j, 0)),
        ],
        out_specs=(pl.BlockSpec((tm, d), lambda j: (0, 0)), pl.BlockSpec((d, tf), lambda j: (0, j)),
                   pl.BlockSpec((d, tf), lambda j: (0, j)), pl.BlockSpec((tf, d), lambda j: (j, 0))),
        scratch_shapes=[pltpu.VMEM((tm, d), BF16)],
        compiler_params=_params(("arbitrary",)),
        name="ffn_first",
    )(x, g3, w_in, w_in, w_out)
    wcol = lambda i, j: (0, jnp.where(i == 0, 0, j))
    return pl.pallas_call(
        _ffn_rest_kernel,
        out_shape=jax.ShapeDtypeStruct((m, d), F32),
        grid=(m // tm, n_f2),
        in_specs=[
            pl.BlockSpec((tm, d), lambda i, j: (jnp.maximum(i, 1), 0)),
            pl.BlockSpec((rows, d), lambda i, j: (jnp.where(i == 0, jnp.minimum(j, n_copy - 1), n_copy - 1), 0)),
            pl.BlockSpec((None, 1, d), lambda i, j: (layer, 0, 0)),
            pl.BlockSpec((d, tf2), wcol),
            pl.BlockSpec((d, tf2), wcol),
            pl.BlockSpec((tf2, d), lambda i, j: (jnp.where(i == 0, 0, j), 0)),
        ],
        out_specs=pl.BlockSpec((tm, d), lambda i, j: (i, 0)),
        scratch_shapes=[pltpu.VMEM((tm, d), BF16)],
        compiler_params=_params(("parallel", "arbitrary")),
        name="ffn_rest",
    )(x, h0, g3, wgb, wub, wob)


def _swap_halves_cols(w):
    n = w.shape[-1] // 2
    return jnp.concatenate([w[..., n:], w[..., :n]], axis=-1)


def _row_sum_mxu(sq, ones):
    return _dot(_bf(sq), ones)


def _mla_proj_kernel(x_ref, pos_ref, mixg_ref, win_ref, qlg_ref, kvlg_ref, wuq_ref, wukv_ref,
                     qg_ref, kg_ref, ones_ref, q_ref, k_ref, v_ref):
    xn = _bf(_rms(x_ref[...], mixg_ref[...]))
    lat = _dot(xn, win_ref[...])
    c_q = _bf(_rms(lat[:, :Q_LORA], qlg_ref[...]))
    c_kv = _bf(_rms(lat[:, Q_LORA:Q_LORA + KV_LORA], kvlg_ref[...]))
    k_rope = lat[:, Q_LORA + KV_LORA:]

    tm = x_ref.shape[0]
    half = D_ROPE // 2
    lane = lax.broadcasted_iota(jnp.int32, (tm, LANES), 1)
    inv_freq = jnp.exp((lane % half).astype(F32) * (-2.0 / D_ROPE * math.log(ROPE_THETA)))
    ang = pos_ref[...].astype(F32) * inv_freq
    keep = lane < D_ROPE
    cos = jnp.where(keep, jnp.cos(ang), 0.0)
    sin = jnp.sin(ang)
    sin = jnp.where(keep, jnp.where(lane < half, -sin, sin), 0.0)

    def rope(t):
        return t * cos + pltpu.roll(t, D_ROPE, 1) * sin

    sel = ones_ref[...]
    ones_rope = ones_ref[D_NOPE:2 * D_NOPE, :LANES]

    qg = qg_ref[...]
    kg = kg_ref[...]
    k_rot = rope(k_rope * kg[:, D_NOPE:])
    kr_ss = _row_sum_mxu(k_rope * k_rope, ones_rope)
    scale = QK_DIM ** -0.5
    for h in range(N_HEADS):
        cols = slice(h * QK_PAD, (h + 1) * QK_PAD)
        qh = _dot(c_q, wuq_ref[:, cols])
        kvh = _dot(c_kv, wukv_ref[:, cols])
        kn = kvh[:, :D_NOPE]
        ss = _row_sum_mxu(jnp.concatenate([qh * qh, kn * kn], axis=1), sel)
        rq = lax.rsqrt(ss[:, :LANES] * (1.0 / QK_DIM) + EPS) * scale
        rk = lax.rsqrt((ss[:, LANES:] + kr_ss) * (1.0 / QK_DIM) + EPS)
        qh = qh * qg
        q_ref[0, h, :, :D_NOPE] = _bf(qh[:, :D_NOPE] * rq)
        q_ref[0, h, :, D_NOPE:] = _bf(rope(qh[:, D_NOPE:]) * rq)
        k_ref[0, h, :, :D_NOPE] = _bf(kn * kg[:, :D_NOPE] * rk)
        k_ref[0, h, :, D_NOPE:] = _bf(k_rot * rk)
        v_ref[0, h, :, :] = _bf(kvh[:, D_NOPE:])


def _mla_proj(x, pos, mix_g, w_in, q_lat_g, kv_lat_g, w_uq, w_ukv, q_gain, k_gain, *, batch, tm=512):
    m, d = x.shape
    seq = m // batch
    n_s = seq // tm
    lat_w = Q_LORA + KV_LORA + LANES
    w_in_p = _bf(jnp.concatenate([w_in, _swap_halves_cols(w_in[:, Q_LORA + KV_LORA:])], axis=1))
    w_uq_h = w_uq.reshape(Q_LORA, N_HEADS, QK_DIM)
    w_uq_p = _bf(jnp.concatenate([w_uq_h, _swap_halves_cols(w_uq_h[..., D_NOPE:])], axis=-1)
                 .reshape(Q_LORA, N_HEADS * QK_PAD))
    qg_p = jnp.concatenate([q_gain, _swap_halves_cols(q_gain[D_NOPE:])]).reshape(1, QK_PAD)
    kg_p = jnp.concatenate([k_gain, _swap_halves_cols(k_gain[D_NOPE:])]).reshape(1, QK_PAD)
    sel_r = lax.broadcasted_iota(jnp.int32, (QK_PAD + D_NOPE, 2 * LANES), 0)
    sel_c = lax.broadcasted_iota(jnp.int32, (QK_PAD + D_NOPE, 2 * LANES), 1)
    ones = jnp.where(sel_c < LANES, sel_r < QK_DIM, sel_r >= QK_PAD).astype(BF16)
    hm = lambda w: pl.BlockSpec((1, N_HEADS, tm, w), lambda i: (i // n_s, 0, i % n_s, 0))
    return pl.pallas_call(
        _mla_proj_kernel,
        out_shape=(jax.ShapeDtypeStruct((batch, N_HEADS, seq, QK_PAD), BF16),
                   jax.ShapeDtypeStruct((batch, N_HEADS, seq, QK_PAD), BF16),
                   jax.ShapeDtypeStruct((batch, N_HEADS, seq, D_V), BF16)),
        grid=(m // tm,),
        in_specs=[
            pl.BlockSpec((tm, d), lambda i: (i, 0)),
            pl.BlockSpec((tm, 1), lambda i: (i, 0)),
            _const_spec((1, d)),
            _const_spec((d, lat_w)),
            _const_spec((1, Q_LORA)),
            _const_spec((1, KV_LORA)),
            _const_spec((Q_LORA, N_HEADS * QK_PAD)),
            _const_spec((KV_LORA, N_HEADS * (D_NOPE + D_V))),
            _const_spec((1, QK_PAD)),
            _const_spec((1, QK_PAD)),
            _const_spec((QK_PAD + D_NOPE, 2 * LANES)),
        ],
        out_specs=(hm(QK_PAD), hm(QK_PAD), hm(D_V)),
        compiler_params=_params(("parallel",)),
        name="mla_proj",
    )(x, pos.reshape(m, 1), mix_g.reshape(1, d), w_in_p, q_lat_g.reshape(1, Q_LORA),
      kv_lat_g.reshape(1, KV_LORA), w_uq_p, _bf(w_ukv), qg_p, kg_p, ones)


def _attn_kernel(q_ref, k_ref, v_ref, o_ref, *, seq, tq, hb):
    def scores(q, k):
        return lax.dot_general(q, k, (((1,), (1,)), ((), ())), preferred_element_type=F32)

    row = lax.broadcasted_iota(jnp.int32, (tq, tq), 0)
    col = lax.broadcasted_iota(jnp.int32, (tq, tq), 1)
    causal = col <= row
    ones = jnp.ones((seq, LANES), BF16)
    for hh in range(hb):
        v1 = jnp.concatenate([v_ref[0, hh], ones], axis=1)
        for qi in range(seq // tq):
            lo = qi * tq
            q = q_ref[0, hh, lo:lo + tq, :]
            s_d = jnp.where(causal, scores(q, k_ref[0, hh, lo:lo + tq, :]), -jnp.inf)
            m = jnp.max(s_d, axis=-1, keepdims=True)
            if qi:
                s_o = scores(q, k_ref[0, hh, :lo, :])
                m = jnp.maximum(m, jnp.max(s_o, axis=-1, keepdims=True))
            o = _dot(_bf(jnp.exp(s_d - m)), v1[lo:lo + tq])
            if qi:
                o = o + _dot(_bf(jnp.exp(s_o - m)), v1[:lo])
            o_ref[0, lo:lo + tq, hh * D_V:(hh + 1) * D_V] = _bf(o[:, :D_V] / o[:, D_V:])


def _attention(q, k, v, *, tq=512, hb=4):
    b, h, s, _ = q.shape
    return pl.pallas_call(
        functools.partial(_attn_kernel, seq=s, tq=tq, hb=hb),
        out_shape=jax.ShapeDtypeStruct((b, s, h * D_V), BF16),
        grid=(b, h // hb),
        in_specs=[
            pl.BlockSpec((1, hb, s, QK_PAD), lambda i, j: (i, j, 0, 0)),
            pl.BlockSpec((1, hb, s, QK_PAD), lambda i, j: (i, j, 0, 0)),
            pl.BlockSpec((1, hb, s, D_V), lambda i, j: (i, j, 0, 0)),
        ],
        out_specs=pl.BlockSpec((1, s, hb * D_V), lambda i, j: (i, 0, j)),
        compiler_params=_params(("parallel", "parallel")),
        name="attention",
    )(q, k, v)


def _proj_res_kernel(a_ref, w_ref, h_ref, o_ref):
    o_ref[...] = h_ref[...] + _dot(a_ref[...], _bf(w_ref[...]))


def _proj_residual(a, w, h, *, tm=512):
    m, d = h.shape
    k = a.shape[1]
    return pl.pallas_call(
        _proj_res_kernel,
        out_shape=jax.ShapeDtypeStruct((m, d), F32),
        grid=(m // tm,),
        in_specs=[
            pl.BlockSpec((tm, k), lambda i: (i, 0)),
            _const_spec((k, d)),
            pl.BlockSpec((tm, d), lambda i: (i, 0)),
        ],
        out_specs=pl.BlockSpec((tm, d), lambda i: (i, 0)),
        compiler_params=_params(("parallel",)),
        name="attn_out_proj",
    )(a, w, h)


def _glu_kernel(x_ref, g_ref, wa_ref, wg_ref, ba_ref, bg_ref, u_ref, xn_ref):
    @pl.when(pl.program_id(1) == 0)
    def _():
        xn_ref[...] = _bf(_rms(x_ref[...], g_ref[...]))

    xn = xn_ref[...]
    a = _dot(xn, _bf(wa_ref[...])) + ba_ref[...]
    g = _dot(xn, _bf(wg_ref[...])) + bg_ref[...]
    u_ref[...] = _bf(a * jax.nn.sigmoid(g))


def _glu(x, norm_g, w, b, *, tm=1024, tn=512):
    m, d = x.shape
    c = w.shape[1] // 2
    n_c = c // tn
    b2 = b.reshape(1, 2 * c)
    return pl.pallas_call(
        _glu_kernel,
        out_shape=jax.ShapeDtypeStruct((m, c), BF16),
        grid=(m // tm, n_c),
        in_specs=[
            pl.BlockSpec((tm, d), lambda i, j: (i, 0)),
            pl.BlockSpec((1, d), lambda i, j: (0, 0)),
            pl.BlockSpec((d, tn), lambda i, j: (0, j)),
            pl.BlockSpec((d, tn), lambda i, j: (0, j + n_c)),
            pl.BlockSpec((1, tn), lambda i, j: (0, j)),
            pl.BlockSpec((1, tn), lambda i, j: (0, j + n_c)),
        ],
        out_specs=pl.BlockSpec((tm, tn), lambda i, j: (i, j)),
        scratch_shapes=[pltpu.VMEM((tm, d), BF16)],
        compiler_params=_params(("parallel", "arbitrary")),
        name="conv_glu",
    )(x, norm_g.reshape(1, d), w, w, b2, b2)


def _dft_tables(ts):
    n = CONV_HIST + ts
    n_bins = n // 2 + 1
    n_bins_p = -(-n_bins // LANES) * LANES
    f = np.arange(n_bins_p, dtype=np.float64)[:, None]
    valid = (f < n_bins).astype(np.float64)
    r = np.arange(n, dtype=np.float64)[None, :]
    ang = 2.0 * np.pi * f * r / n
    fwd = np.concatenate([np.cos(ang) * valid, -np.sin(ang) * valid], axis=0)
    weight = np.where((f == 0) | (f == n // 2), 1.0, 2.0) * valid / n
    ang_o = ang[:, CONV_HIST:]
    inv = np.concatenate([(np.cos(ang_o) * weight).T, (-np.sin(ang_o) * weight).T], axis=1)
    delay = (CONV_WIDTH - 1) - np.arange(CONV_TAPS_P, dtype=np.float64)[None, :]
    tap_ok = (np.arange(CONV_TAPS_P) < CONV_WIDTH).astype(np.float64)[None, :]
    ang_t = 2.0 * np.pi * f * delay / n
    taps = np.concatenate([np.cos(ang_t) * valid * tap_ok, -np.sin(ang_t) * valid * tap_ok], axis=0)
    return fwd, inv, taps


def _conv_kernel(u_ref, hist_ref, fwd_ref, inv_ref, taps_ref, wdw_ref, bdw_ref, lng_ref, lnb_ref,
                 w2_ref, h_ref, o_ref, spec_ref, y_ref, *, cn):
    @pl.when((pl.program_id(0) == 0) & (pl.program_id(1) == 0))
    def _():
        spec_ref[...] = jnp.dot(taps_ref[...], wdw_ref[...], precision=lax.Precision.HIGHEST,
                                preferred_element_type=F32)

    n_bins_p = fwd_ref.shape[0] // 2
    have_hist = pl.program_id(1) > 0
    c = u_ref.shape[2]
    for c0 in range(0, c, cn):
        cols = slice(c0, c0 + cn)
        hist = hist_ref[0, :, cols]
        x = jnp.concatenate([jnp.where(have_hist, hist, jnp.zeros_like(hist)),
                             u_ref[0, :, cols]], axis=0)
        xs = _dot(fwd_ref[...], x)
        xr, xi = xs[:n_bins_p], xs[n_bins_p:]
        fr, fi = spec_ref[:n_bins_p, cols], spec_ref[n_bins_p:, cols]
        ys = jnp.concatenate([_bf(xr * fr - xi * fi), _bf(xr * fi + xi * fr)], axis=0)
        y_ref[:, cols] = _dot(inv_ref[...], ys) + bdw_ref[:, cols]

    y = y_ref[...]
    mu = jnp.mean(y, axis=-1, keepdims=True)
    yc = y - mu
    yn = yc * lax.rsqrt(jnp.mean(yc * yc, axis=-1, keepdims=True) + EPS) * lng_ref[...] + lnb_ref[...]
    act = _bf(yn * jax.nn.sigmoid(yn))
    o_ref[0] = h_ref[0] + _dot(act, _bf(w2_ref[...]))


def _conv_module(u, w_dw, b_dw, ln_g, ln_b, w2, h, *, ts=256, cn=1024):
    b, s, c = u.shape
    d = w2.shape[1]
    r = ts // CONV_HIST
    fwd, inv, taps = _dft_tables(ts)
    n_spec = fwd.shape[0]
    row = lambda i, j: (i, j, 0)
    return pl.pallas_call(
        functools.partial(_conv_kernel, cn=cn),
        out_shape=jax.ShapeDtypeStruct((b, s, d), F32),
        grid=(b, s // ts),
        in_specs=[
            pl.BlockSpec((1, ts, c), row),
            pl.BlockSpec((1, CONV_HIST, c), lambda i, j: (i, jnp.maximum(j * r - 1, 0), 0)),
            _const_spec(fwd.shape),
            _const_spec(inv.shape),
            _const_spec(taps.shape),
            _const_spec((CONV_TAPS_P, c)),
            _const_spec((1, c)),
            _const_spec((1, c)),
            _const_spec((1, c)),
            _const_spec((c, d)),
            pl.BlockSpec((1, ts, d), row),
        ],
        out_specs=pl.BlockSpec((1, ts, d), row),
        scratch_shapes=[pltpu.VMEM((n_spec, c), F32), pltpu.VMEM((ts, c), F32)],
        compiler_params=_params(("arbitrary", "arbitrary")),
        name="conv_module",
    )(u, u, jnp.asarray(fwd, BF16), jnp.asarray(inv, BF16), jnp.asarray(taps, F32),
      jnp.pad(w_dw, ((0, CONV_TAPS_P - CONV_WIDTH), (0, 0))), b_dw.reshape(1, c),
      ln_g.reshape(1, c), ln_b.reshape(1, c), w2, h)


def _ple_kernel(h_ref, p_ref, wp_ref, eg_ref, gg_ref, wg_ref, o_ref):
    h = h_ref[...]
    e = _rms(_dot(_bf(p_ref[...]), _bf(wp_ref[...])), eg_ref[...])
    gate = jax.nn.sigmoid(_dot(_bf(_rms(h, gg_ref[...])), _bf(wg_ref[...])))
    o_ref[...] = h + e * gate


def _ple(h, p, w_proj, e_norm, gate_norm, w_gate, layer, *, tm=512):
    m, d = h.shape
    return pl.pallas_call(
        _ple_kernel,
        out_shape=jax.ShapeDtypeStruct((m, d), F32),
        grid=(m // tm,),
        in_specs=[
            pl.BlockSpec((tm, d), lambda i: (i, 0)),
            pl.BlockSpec((None, tm, D_PLE), lambda i: (layer, i, 0)),
            _const_spec((D_PLE, d), layer),
            _const_spec((1, d), layer),
            _const_spec((1, d), layer),
            _const_spec((d, d), layer),
        ],
        out_specs=pl.BlockSpec((tm, d), lambda i: (i, 0)),
        compiler_params=_params(("parallel",)),
        name="ple",
    )(h, p, w_proj, e_norm.reshape(-1, 1, d), gate_norm.reshape(-1, 1, d), w_gate)


def kernel(x, p, positions, ffn_a_norm, ffn_a_w_in, ffn_a_w_out, ffn_b_norm, ffn_b_w_in, ffn_b_w_out,
           mix_norm, mla_w_in, mla_q_lat_norm, mla_kv_lat_norm, mla_w_uq, mla_w_ukv, mla_q_gain,
           mla_k_gain, mla_w_o, conv_w_pw1, conv_b_pw1, conv_w_dw, conv_b_dw, conv_ln_g, conv_ln_b,
           conv_w_pw2, ple_w_proj, ple_norm, ple_gate_norm, ple_w_gate):
    batch, seq, d = x.shape
    m = batch * seq
    depth = ffn_a_norm.shape[0]
    h = x.reshape(m, d)
    pos = positions.reshape(m)
    p2 = p.reshape(depth, m, -1)
    for i in range(depth):
        h = _ffn(h, ffn_a_norm, ffn_a_w_in, ffn_a_w_out, i)
        j = i // 2
        if i % 2 == 0:
            q, k, v = _mla_proj(h, pos, mix_norm[i], mla_w_in[j], mla_q_lat_norm[j], mla_kv_lat_norm[j],
                                mla_w_uq[j], mla_w_ukv[j], mla_q_gain[j], mla_k_gain[j], batch=batch)
            o = _attention(q, k, v)
            h = _proj_residual(o.reshape(m, N_HEADS * D_V), mla_w_o[j], h)
        else:
            u = _glu(h, mix_norm[i], conv_w_pw1[j], conv_b_pw1[j])
            h = _conv_module(u.reshape(batch, seq, -1), conv_w_dw[j], conv_b_dw[j], conv_ln_g[j],
                             conv_ln_b[j], conv_w_pw2[j], h.reshape(batch, seq, d)).reshape(m, d)
        h = _ffn(h, ffn_b_norm, ffn_b_w_in, ffn_b_w_out, i)
        h = _ple(h, p2, ple_w_proj, ple_norm, ple_gate_norm, ple_w_gate, i)
    return h.reshape(batch, seq, d)
```

```python
import functools
import math

import jax
import jax.numpy as jnp
import numpy as np
from jax import lax
from jax.experimental import pallas as pl
from jax.experimental.pallas import tpu as pltpu

D_MODEL = 2048
N_HEADS = 16
Q_LORA = 512
KV_LORA = 512
D_NOPE = 128
D_ROPE = 64
D_V = 128
QK_DIM = D_NOPE + D_ROPE
ROPE_THETA = 10000.0
CONV_WIDTH = 31
D_FF = 5632
D_PLE = 256
EPS = 1e-6
FFN_RESIDUAL_WEIGHT = 0.5

LANES = 128
V7X_VMEM_LIMIT_BYTES = 60 * 1024 * 1024

QK_PAD = 2 * LANES
CONV_HIST = LANES
CONV_TAPS_P = 32

F32 = jnp.float32
BF16 = jnp.bfloat16


def _bf(x):
    return x.astype(BF16)


def _dot(a, b):
    return jnp.dot(a, b, preferred_element_type=F32)


def _rms(x, g):
    ms = jnp.mean(x * x, axis=-1, keepdims=True)
    return x * lax.rsqrt(ms + EPS) * g


def _params(sem):
    return pltpu.CompilerParams(dimension_semantics=sem,
                                vmem_limit_bytes=V7X_VMEM_LIMIT_BYTES)


def _const_spec(shape, layer=None):
    nd = len(shape)
    if layer is None:
        return pl.BlockSpec(shape, lambda *_: (0,) * nd, pipeline_mode=pl.Buffered(1))
    return pl.BlockSpec((None,) + shape, lambda *_: (layer,) + (0,) * nd,
                        pipeline_mode=pl.Buffered(1))


def _ffn_first_kernel(x_ref, g_ref, wg_ref, wu_ref, wo_ref, o_ref, wgb_ref, wub_ref, wob_ref, xn_ref):
    @pl.when(pl.program_id(0) == 0)
    def _():
        x = x_ref[...]
        xn_ref[...] = _bf(_rms(x, g_ref[...]))
        o_ref[...] = x

    wg, wu, wo = _bf(wg_ref[...]), _bf(wu_ref[...]), _bf(wo_ref[...])
    wgb_ref[...] = wg
    wub_ref[...] = wu
    wob_ref[...] = wo
    xn = xn_ref[...]
    g = _dot(xn, wg)
    u = _dot(xn, wu)
    a = _bf(g * jax.nn.sigmoid(g) * u)
    o_ref[...] += FFN_RESIDUAL_WEIGHT * _dot(a, wo)


def _ffn_rest_kernel(x_ref, h0_ref, g_ref, wg_ref, wu_ref, wo_ref, o_ref, xn_ref):
    i = pl.program_id(0)
    j = pl.program_id(1)
    rows = h0_ref.shape[0]

    @pl.when((i == 0) & (j < o_ref.shape[0] // rows))
    def _():
        o_ref[pl.ds(pl.multiple_of(j * rows, rows), rows), :] = h0_ref[...]

    @pl.when((i > 0) & (j == 0))
    def _():
        x = x_ref[...]
        xn_ref[...] = _bf(_rms(x, g_ref[...]))
        o_ref[...] = x

    @pl.when(i > 0)
    def _():
        xn = xn_ref[...]
        g = _dot(xn, wg_ref[...])
        u = _dot(xn, wu_ref[...])
        a = _bf(g * jax.nn.sigmoid(g) * u)
        o_ref[...] += FFN_RESIDUAL_WEIGHT * _dot(a, wo_ref[...])


def _ffn(x, norm_g, w_in, w_out, layer, *, tm=1024, tf=256, tf2=512, rows=128):
    m, d = x.shape
    n_f = D_FF // tf
    n_f2 = D_FF // tf2
    n_copy = tm // rows
    g3 = norm_g.reshape(-1, 1, d)
    h0, wgb, wub, wob = pl.pallas_call(
        _ffn_first_kernel,
        out_shape=(jax.ShapeDtypeStruct((tm, d), F32), jax.ShapeDtypeStruct((d, D_FF), BF16),
                   jax.ShapeDtypeStruct((d, D_FF), BF16), jax.ShapeDtypeStruct((D_FF, d), BF16)),
        grid=(n_f,),
        in_specs=[
            pl.BlockSpec((tm, d), lambda j: (0, 0)),
            pl.BlockSpec((None, 1, d), lambda j: (layer, 0, 0)),
            pl.BlockSpec((None, d, tf), lambda j: (layer, 0, j)),
            pl.BlockSpec((None, d, tf), lambda j: (layer, 0, j + n_f)),
            pl.BlockSpec((None, tf, d), lambda j: (layer, j, 0)),
        ],
        out_specs=(pl.BlockSpec((tm, d), lambda j: (0, 0)), pl.BlockSpec((d, tf), lambda j: (0, j)),
                   pl.BlockSpec((d, tf), lambda j: (0, j)), pl.BlockSpec((tf, d), lambda j: (j, 0))),
        scratch_shapes=[pltpu.VMEM((tm, d), BF16)],
        compiler_params=_params(("arbitrary",)),
        name="ffn_first",
    )(x, g3, w_in, w_in, w_out)
    wcol = lambda i, j: (0, jnp.where(i == 0, 0, j))
    return pl.pallas_call(
        _ffn_rest_kernel,
        out_shape=jax.ShapeDtypeStruct((m, d), F32),
        grid=(m // tm, n_f2),
        in_specs=[
            pl.BlockSpec((tm, d), lambda i, j: (jnp.maximum(i, 1), 0)),
            pl.BlockSpec((rows, d), lambda i, j: (jnp.where(i == 0, jnp.minimum(j, n_copy - 1), n_copy - 1), 0)),
            pl.BlockSpec((None, 1, d), lambda i, j: (layer, 0, 0)),
            pl.BlockSpec((d, tf2), wcol),
            pl.BlockSpec((d, tf2), wcol),
            pl.BlockSpec((tf2, d), lambda i, j: (jnp.where(i == 0, 0, j), 0)),
        ],
        out_specs=pl.BlockSpec((tm, d), lambda i, j: (i, 0)),
        scratch_shapes=[pltpu.VMEM((tm, d), BF16)],
        compiler_params=_params(("parallel", "arbitrary")),
        name="ffn_rest",
    )(x, h0, g3, wgb, wub, wob)


def _swap_halves_cols(w):
    n = w.shape[-1] // 2
    return jnp.concatenate([w[..., n:], w[..., :n]], axis=-1)


def _row_sum_mxu(sq, ones):
    return _dot(_bf(sq), ones)


def _mla_proj_kernel(x_ref, pos_ref, mixg_ref, win_ref, qlg_ref, kvlg_ref, wuq_ref, wukv_ref,
                     qg_ref, kg_ref, ones_ref, q_ref, k_ref, v_ref):
    xn = _bf(_rms(x_ref[...], mixg_ref[...]))
    lat = _dot(xn, win_ref[...])
    c_q = _bf(_rms(lat[:, :Q_LORA], qlg_ref[...]))
    c_kv = _bf(_rms(lat[:, Q_LORA:Q_LORA + KV_LORA], kvlg_ref[...]))
    k_rope = lat[:, Q_LORA + KV_LORA:]

    tm = x_ref.shape[0]
    half = D_ROPE // 2
    lane = lax.broadcasted_iota(jnp.int32, (tm, LANES), 1)
    inv_freq = jnp.exp((lane % half).astype(F32) * (-2.0 / D_ROPE * math.log(ROPE_THETA)))
    ang = pos_ref[...].astype(F32) * inv_freq
    keep = lane < D_ROPE
    cos = jnp.where(keep, jnp.cos(ang), 0.0)
    sin = jnp.sin(ang)
    sin = jnp.where(keep, jnp.where(lane < half, -sin, sin), 0.0)

    def rope(t):
        return t * cos + pltpu.roll(t, D_ROPE, 1) * sin

    sel = ones_ref[...]
    ones_rope = ones_ref[D_NOPE:2 * D_NOPE, :LANES]

    qg = qg_ref[...]
    kg = kg_ref[...]
    k_rot = rope(k_rope * kg[:, D_NOPE:])
    kr_ss = _row_sum_mxu(k_rope * k_rope, ones_rope)
    scale = QK_DIM ** -0.5
    for h in range(N_HEADS):
        cols = slice(h * QK_PAD, (h + 1) * QK_PAD)
        qh = _dot(c_q, wuq_ref[:, cols])
        kvh = _dot(c_kv, wukv_ref[:, cols])
        kn = kvh[:, :D_NOPE]
        ss = _row_sum_mxu(jnp.concatenate([qh * qh, kn * kn], axis=1), sel)
        rq = lax.rsqrt(ss[:, :LANES] * (1.0 / QK_DIM) + EPS) * scale
        rk = lax.rsqrt((ss[:, LANES:] + kr_ss) * (1.0 / QK_DIM) + EPS)
        qh = qh * qg
        q_ref[0, h, :, :D_NOPE] = _bf(qh[:, :D_NOPE] * rq)
        q_ref[0, h, :, D_NOPE:] = _bf(rope(qh[:, D_NOPE:]) * rq)
        k_ref[0, h, :, :D_NOPE] = _bf(kn * kg[:, :D_NOPE] * rk)
        k_ref[0, h, :, D_NOPE:] = _bf(k_rot * rk)
        v_ref[0, h, :, :] = _bf(kvh[:, D_NOPE:])


def _mla_proj(x, pos, mix_g, w_in, q_lat_g, kv_lat_g, w_uq, w_ukv, q_gain, k_gain, *, batch, tm=512):
    m, d = x.shape
    seq = m // batch
    n_s = seq // tm
    lat_w = Q_LORA + KV_LORA + LANES
    w_in_p = _bf(jnp.concatenate([w_in, _swap_halves_cols(w_in[:, Q_LORA + KV_LORA:])], axis=1))
    w_uq_h = w_uq.reshape(Q_LORA, N_HEADS, QK_DIM)
    w_uq_p = _bf(jnp.concatenate([w_uq_h, _swap_halves_cols(w_uq_h[..., D_NOPE:])], axis=-1)
                 .reshape(Q_LORA, N_HEADS * QK_PAD))
    qg_p = jnp.concatenate([q_gain, _swap_halves_cols(q_gain[D_NOPE:])]).reshape(1, QK_PAD)
    kg_p = jnp.concatenate([k_gain, _swap_halves_cols(k_gain[D_NOPE:])]).reshape(1, QK_PAD)
    sel_r = lax.broadcasted_iota(jnp.int32, (QK_PAD + D_NOPE, 2 * LANES), 0)
    sel_c = lax.broadcasted_iota(jnp.int32, (QK_PAD + D_NOPE, 2 * LANES), 1)
    ones = jnp.where(sel_c < LANES, sel_r < QK_DIM, sel_r >= QK_PAD).astype(BF16)
    hm = lambda w: pl.BlockSpec((1, N_HEADS, tm, w), lambda i: (i // n_s, 0, i % n_s, 0))
    return pl.pallas_call(
        _mla_proj_kernel,
        out_shape=(jax.ShapeDtypeStruct((batch, N_HEADS, seq, QK_PAD), BF16),
                   jax.ShapeDtypeStruct((batch, N_HEADS, seq, QK_PAD), BF16),
                   jax.ShapeDtypeStruct((batch, N_HEADS, seq, D_V), BF16)),
        grid=(m // tm,),
        in_specs=[
            pl.BlockSpec((tm, d), lambda i: (i, 0)),
            pl.BlockSpec((tm, 1), lambda i: (i, 0)),
            _const_spec((1, d)),
            _const_spec((d, lat_w)),
            _const_spec((1, Q_LORA)),
            _const_spec((1, KV_LORA)),
            _const_spec((Q_LORA, N_HEADS * QK_PAD)),
            _const_spec((KV_LORA, N_HEADS * (D_NOPE + D_V))),
            _const_spec((1, QK_PAD)),
            _const_spec((1, QK_PAD)),
            _const_spec((QK_PAD + D_NOPE, 2 * LANES)),
        ],
        out_specs=(hm(QK_PAD), hm(QK_PAD), hm(D_V)),
        compiler_params=_params(("parallel",)),
        name="mla_proj",
    )(x, pos.reshape(m, 1), mix_g.reshape(1, d), w_in_p, q_lat_g.reshape(1, Q_LORA),
      kv_lat_g.reshape(1, KV_LORA), w_uq_p, _bf(w_ukv), qg_p, kg_p, ones)


def _attn_kernel(q_ref, k_ref, v_ref, o_ref, *, seq, tq, hb):
    def scores(q, k):
        return lax.dot_general(q, k, (((1,), (1,)), ((), ())), preferred_element_type=F32)

    row = lax.broadcasted_iota(jnp.int32, (tq, tq), 0)
    col = lax.broadcasted_iota(jnp.int32, (tq, tq), 1)
    causal = col <= row
    ones = jnp.ones((seq, LANES), BF16)
    for hh in range(hb):
        v1 = jnp.concatenate([v_ref[0, hh], ones], axis=1)
        for qi in range(seq // tq):
            lo = qi * tq
            q = q_ref[0, hh, lo:lo + tq, :]
            s_d = jnp.where(causal, scores(q, k_ref[0, hh, lo:lo + tq, :]), -jnp.inf)
            m = jnp.max(s_d, axis=-1, keepdims=True)
            if qi:
                s_o = scores(q, k_ref[0, hh, :lo, :])
                m = jnp.maximum(m, jnp.max(s_o, axis=-1, keepdims=True))
            o = _dot(_bf(jnp.exp(s_d - m)), v1[lo:lo + tq])
            if qi:
                o = o + _dot(_bf(jnp.exp(s_o - m)), v1[:lo])
            o_ref[0, lo:lo + tq, hh * D_V:(hh + 1) * D_V] = _bf(o[:, :D_V] / o[:, D_V:])


def _attention(q, k, v, *, tq=512, hb=4):
    b, h, s, _ = q.shape
    return pl.pallas_call(
        functools.partial(_attn_kernel, seq=s, tq=tq, hb=hb),
        out_shape=jax.ShapeDtypeStruct((b, s, h * D_V), BF16),
        grid=(b, h // hb),
        in_specs=[
            pl.BlockSpec((1, hb, s, QK_PAD), lambda i, j: (i, j, 0, 0)),
            pl.BlockSpec((1, hb, s, QK_PAD), lambda i, j: (i, j, 0, 0)),
            pl.BlockSpec((1, hb, s, D_V), lambda i, j: (i, j, 0, 0)),
        ],
        out_specs=pl.BlockSpec((1, s, hb * D_V), lambda i, j: (i, 0, j)),
        compiler_params=_params(("parallel", "parallel")),
        name="attention",
    )(q, k, v)


def _proj_res_kernel(a_ref, w_ref, h_ref, o_ref, wb_ref):
    @pl.when(pl.program_id(0) == 0)
    def _():
        wb_ref[...] = _bf(w_ref[...])

    o_ref[...] = h_ref[...] + _dot(a_ref[...], wb_ref[...])


def _proj_residual(a, w, h, *, tm=512):
    m, d = h.shape
    k = a.shape[1]
    return pl.pallas_call(
        _proj_res_kernel,
        out_shape=jax.ShapeDtypeStruct((m, d), F32),
        grid=(m // tm,),
        in_specs=[
            pl.BlockSpec((tm, k), lambda i: (i, 0)),
            _const_spec((k, d)),
            pl.BlockSpec((tm, d), lambda i: (i, 0)),
        ],
        out_specs=pl.BlockSpec((tm, d), lambda i: (i, 0)),
        scratch_shapes=[pltpu.VMEM((k, d), BF16)],
        compiler_params=_params(("arbitrary",)),
        name="attn_out_proj",
    )(a, w, h)


def _glu_kernel(x_ref, g_ref, wa_ref, wg_ref, ba_ref, bg_ref, u_ref, xn_ref):
    @pl.when(pl.program_id(1) == 0)
    def _():
        xn_ref[...] = _bf(_rms(x_ref[...], g_ref[...]))

    xn = xn_ref[...]
    a = _dot(xn, _bf(wa_ref[...])) + ba_ref[...]
    g = _dot(xn, _bf(wg_ref[...])) + bg_ref[...]
    u_ref[...] = _bf(a * jax.nn.sigmoid(g))


def _glu(x, norm_g, w, b, *, tm=1024, tn=512):
    m, d = x.shape
    c = w.shape[1] // 2
    n_c = c // tn
    b2 = b.reshape(1, 2 * c)
    return pl.pallas_call(
        _glu_kernel,
        out_shape=jax.ShapeDtypeStruct((m, c), BF16),
        grid=(m // tm, n_c),
        in_specs=[
            pl.BlockSpec((tm, d), lambda i, j: (i, 0)),
            pl.BlockSpec((1, d), lambda i, j: (0, 0)),
            pl.BlockSpec((d, tn), lambda i, j: (0, j)),
            pl.BlockSpec((d, tn), lambda i, j: (0, j + n_c)),
            pl.BlockSpec((1, tn), lambda i, j: (0, j)),
            pl.BlockSpec((1, tn), lambda i, j: (0, j + n_c)),
        ],
        out_specs=pl.BlockSpec((tm, tn), lambda i, j: (i, j)),
        scratch_shapes=[pltpu.VMEM((tm, d), BF16)],
        compiler_params=_params(("parallel", "arbitrary")),
        name="conv_glu",
    )(x, norm_g.reshape(1, d), w, w, b2, b2)


def _dft_tables(ts):
    n = CONV_HIST + ts
    n_bins = n // 2 + 1
    n_bins_p = -(-n_bins // LANES) * LANES
    f = np.arange(n_bins_p, dtype=np.float64)[:, None]
    valid = (f < n_bins).astype(np.float64)
    r = np.arange(n, dtype=np.float64)[None, :]
    ang = 2.0 * np.pi * f * r / n
    fwd = np.concatenate([np.cos(ang) * valid, -np.sin(ang) * valid], axis=0)
    weight = np.where((f == 0) | (f == n // 2), 1.0, 2.0) * valid / n
    ang_o = ang[:, CONV_HIST:]
    inv = np.concatenate([(np.cos(ang_o) * weight).T, (-np.sin(ang_o) * weight).T], axis=1)
    delay = (CONV_WIDTH - 1) - np.arange(CONV_TAPS_P, dtype=np.float64)[None, :]
    tap_ok = (np.arange(CONV_TAPS_P) < CONV_WIDTH).astype(np.float64)[None, :]
    ang_t = 2.0 * np.pi * f * delay / n
    taps = np.concatenate([np.cos(ang_t) * valid * tap_ok, -np.sin(ang_t) * valid * tap_ok], axis=0)
    return fwd, inv, taps


def _conv_kernel(u_ref, hist_ref, fwd_ref, inv_ref, taps_ref, wdw_ref, bdw_ref, lng_ref, lnb_ref,
                 w2_ref, h_ref, o_ref, spec_ref, y_ref, w2b_ref, *, cn):
    @pl.when((pl.program_id(0) == 0) & (pl.program_id(1) == 0))
    def _():
        spec_ref[...] = jnp.dot(taps_ref[...], wdw_ref[...], precision=lax.Precision.HIGHEST,
                                preferred_element_type=F32)
        w2b_ref[...] = _bf(w2_ref[...])

    n_bins_p = fwd_ref.shape[0] // 2
    have_hist = pl.program_id(1) > 0
    c = u_ref.shape[2]
    for c0 in range(0, c, cn):
        cols = slice(c0, c0 + cn)
        hist = hist_ref[0, :, cols]
        x = jnp.concatenate([jnp.where(have_hist, hist, jnp.zeros_like(hist)),
                             u_ref[0, :, cols]], axis=0)
        xs = _dot(fwd_ref[...], x)
        xr, xi = xs[:n_bins_p], xs[n_bins_p:]
        fr, fi = spec_ref[:n_bins_p, cols], spec_ref[n_bins_p:, cols]
        ys = jnp.concatenate([_bf(xr * fr - xi * fi), _bf(xr * fi + xi * fr)], axis=0)
        y_ref[:, cols] = _dot(inv_ref[...], ys) + bdw_ref[:, cols]

    y = y_ref[...]
    mu = jnp.mean(y, axis=-1, keepdims=True)
    yc = y - mu
    yn = yc * lax.rsqrt(jnp.mean(yc * yc, axis=-1, keepdims=True) + EPS) * lng_ref[...] + lnb_ref[...]
    act = _bf(yn * jax.nn.sigmoid(yn))
    o_ref[0] = h_ref[0] + _dot(act, w2b_ref[...])


def _conv_module(u, w_dw, b_dw, ln_g, ln_b, w2, h, *, ts=256, cn=1024):
    b, s, c = u.shape
    d = w2.shape[1]
    r = ts // CONV_HIST
    fwd, inv, taps = _dft_tables(ts)
    n_spec = fwd.shape[0]
    row = lambda i, j: (i, j, 0)
    return pl.pallas_call(
        functools.partial(_conv_kernel, cn=cn),
        out_shape=jax.ShapeDtypeStruct((b, s, d), F32),
        grid=(b, s // ts),
        in_specs=[
            pl.BlockSpec((1, ts, c), row),
            pl.BlockSpec((1, CONV_HIST, c), lambda i, j: (i, jnp.maximum(j * r - 1, 0), 0)),
            _const_spec(fwd.shape),
            _const_spec(inv.shape),
            _const_spec(taps.shape),
            _const_spec((CONV_TAPS_P, c)),
            _const_spec((1, c)),
            _const_spec((1, c)),
            _const_spec((1, c)),
            _const_spec((c, d)),
            pl.BlockSpec((1, ts, d), row),
        ],
        out_specs=pl.BlockSpec((1, ts, d), row),
        scratch_shapes=[pltpu.VMEM((n_spec, c), F32), pltpu.VMEM((ts, c), F32), pltpu.VMEM((c, d), BF16)],
        compiler_params=_params(("arbitrary", "arbitrary")),
        name="conv_module",
    )(u, u, jnp.asarray(fwd, BF16), jnp.asarray(inv, BF16), jnp.asarray(taps, F32),
      jnp.pad(w_dw, ((0, CONV_TAPS_P - CONV_WIDTH), (0, 0))), b_dw.reshape(1, c),
      ln_g.reshape(1, c), ln_b.reshape(1, c), w2, h)


def _ple_kernel(h_ref, p_ref, wp_ref, eg_ref, gg_ref, wg_ref, o_ref, wpb_ref, wgb_ref):
    @pl.when(pl.program_id(0) == 0)
    def _():
        wpb_ref[...] = _bf(wp_ref[...])
        wgb_ref[...] = _bf(wg_ref[...])

    h = h_ref[...]
    e = _rms(_dot(_bf(p_ref[...]), wpb_ref[...]), eg_ref[...])
    gate = jax.nn.sigmoid(_dot(_bf(_rms(h, gg_ref[...])), wgb_ref[...]))
    o_ref[...] = h + e * gate


def _ple(h, p, w_proj, e_norm, gate_norm, w_gate, layer, *, tm=512):
    m, d = h.shape
    return pl.pallas_call(
        _ple_kernel,
        out_shape=jax.ShapeDtypeStruct((m, d), F32),
        grid=(m // tm,),
        in_specs=[
            pl.BlockSpec((tm, d), lambda i: (i, 0)),
            pl.BlockSpec((None, tm, D_PLE), lambda i: (layer, i, 0)),
            _const_spec((D_PLE, d), layer),
            _const_spec((1, d), layer),
            _const_spec((1, d), layer),
            _const_spec((d, d), layer),
        ],
        out_specs=pl.BlockSpec((tm, d), lambda i: (i, 0)),
        scratch_shapes=[pltpu.VMEM((D_PLE, d), BF16), pltpu.VMEM((d, d), BF16)],
        compiler_params=_params(("arbitrary",)),
        name="ple",
    )(h, p, w_proj, e_norm.reshape(-1, 1, d), gate_norm.reshape(-1, 1, d), w_gate)


def kernel(x, p, positions, ffn_a_norm, ffn_a_w_in, ffn_a_w_out, ffn_b_norm, ffn_b_w_in, ffn_b_w_out,
           mix_norm, mla_w_in, mla_q_lat_norm, mla_kv_lat_norm, mla_w_uq, mla_w_ukv, mla_q_gain,
           mla_k_gain, mla_w_o, conv_w_pw1, conv_b_pw1, conv_w_dw, conv_b_dw, conv_ln_g, conv_ln_b,
           conv_w_pw2, ple_w_proj, ple_norm, ple_gate_norm, ple_w_gate):
    batch, seq, d = x.shape
    m = batch * seq
    depth = ffn_a_norm.shape[0]
    h = x.reshape(m, d)
    pos = positions.reshape(m)
    p2 = p.reshape(depth, m, -1)
    for i in range(depth):
        h = _ffn(h, ffn_a_norm, ffn_a_w_in, ffn_a_w_out, i)
        j = i // 2
        if i % 2 == 0:
            q, k, v = _mla_proj(h, pos, mix_norm[i], mla_w_in[j], mla_q_lat_norm[j], mla_kv_lat_norm[j],
                                mla_w_uq[j], mla_w_ukv[j], mla_q_gain[j], mla_k_gain[j], batch=batch)
            o = _attention(q, k, v)
            h = _proj_residual(o.reshape(m, N_HEADS * D_V), mla_w_o[j], h)
        else:
            u = _glu(h, mix_norm[i], conv_w_pw1[j], conv_b_pw1[j])
            h = _conv_module(u.reshape(batch, seq, -1), conv_w_dw[j], conv_b_dw[j], conv_ln_g[j],
                             conv_ln_b[j], conv_w_pw2[j], h.reshape(batch, seq, d)).reshape(m, d)
        h = _ffn(h, ffn_b_norm, ffn_b_w_in, ffn_b_w_out, i)
        h = _ple(h, p2, ple_w_proj, ple_norm, ple_gate_norm, ple_w_gate, i)
    return h.reshape(batch, seq, d)
```
